```python
import jax, jax.numpy as jnp
from jax import lax
import numpy as np

D_MODEL = 1024
BATCH = 2
SEQ = 16384
DEPTH = 2
DEC_BATCH = 8
DEC_SEQ = 8192
PAST_LEN = 128

N_MIXERS = 2
N_FNET_LAYERS = (DEPTH + 1) // 2
N_CONV_LAYERS = DEPTH // 2
D_MIX = D_MODEL
FNET_GROUPS = 8
FNET_GROUP_DIM = D_MIX // FNET_GROUPS
CONV_WIDTH = 3
D_FF = 2816
N_FFN_PER_LAYER = 2
N_LN_PER_LAYER = 3
ALPHA = float((2 * DEPTH) ** 0.25)
BETA = float((8 * DEPTH) ** -0.25)
LN_EPS = 1e-5

kernel_name = "hybrid_fnet_shortconv_macaron_encoder"


def layer_norm(x, g, b):
    xf = x.astype(jnp.float32)
    mu = jnp.mean(xf, axis=-1, keepdims=True)
    var = jnp.mean(jnp.square(xf - mu), axis=-1, keepdims=True)
    y = (xf - mu) * lax.rsqrt(var + LN_EPS) * g.astype(jnp.float32) + b.astype(jnp.float32)
    return y.astype(x.dtype)


def swiglu_ffn(x, w_gate, w_up, w_down):
    h = jax.nn.silu(jnp.einsum("bsd,df->bsf", x, w_gate)) * jnp.einsum("bsd,df->bsf", x, w_up)
    return jnp.einsum("bsf,fd->bsd", h, w_down)


def fourier_mixer(x, w_in, w_out):
    bsz, s, _ = x.shape
    u = jnp.einsum("bsd,de->bse", x, w_in).reshape(bsz, s, FNET_GROUPS, FNET_GROUP_DIM)
    f = jnp.fft.fft2(u.astype(jnp.float32), axes=(1, 3), norm="ortho").real
    f = f.reshape(bsz, s, D_MIX).astype(x.dtype)
    return jnp.einsum("bse,ed->bsd", f, w_out)


def centred_depthwise_conv(u, w):
    up = jnp.pad(u, ((0, 0), (1, 1), (0, 0)))
    return up[:, :-2] * w[0] + up[:, 1:-1] * w[1] + up[:, 2:] * w[2]


def short_conv_mixer(x, w_in, w_conv, w_out):
    proj = jnp.einsum("bsd,de->bse", x, w_in)
    gate_b, gate_c, h = jnp.split(proj, 3, axis=-1)
    y = gate_b * centred_depthwise_conv(gate_c * h, w_conv)
    return jnp.einsum("bse,ed->bsd", y, w_out)


def trunk(x, ffn_w_gate, ffn_w_up, ffn_w_down, ln_g, ln_b,
          fnet_w_in, fnet_w_out, conv_w_in, conv_w, conv_w_out):
    for i in range(DEPTH):
        x = layer_norm(ALPHA * x + 0.5 * swiglu_ffn(x, ffn_w_gate[i, 0], ffn_w_up[i, 0], ffn_w_down[i, 0]),
                       ln_g[i, 0], ln_b[i, 0])
        j = i // N_MIXERS
        if i % N_MIXERS == 0:
            m = fourier_mixer(x, fnet_w_in[j], fnet_w_out[j])
        else:
            m = short_conv_mixer(x, conv_w_in[j], conv_w[j], conv_w_out[j])
        x = layer_norm(ALPHA * x + m, ln_g[i, 1], ln_b[i, 1])
        x = layer_norm(ALPHA * x + 0.5 * swiglu_ffn(x, ffn_w_gate[i, 1], ffn_w_up[i, 1], ffn_w_down[i, 1]),
                       ln_g[i, 2], ln_b[i, 2])
    return x


def setup_inputs(seed: int = 0) -> dict:
    key = jax.random.key(seed)
    ks = jax.random.split(key, 13)
    f32 = jnp.float32
    d_in = D_MODEL ** -0.5
    return {
        "x_prompt": jax.random.normal(ks[0], (BATCH, SEQ, D_MODEL), f32),
        "x_sample": jax.random.normal(ks[1], (DEC_BATCH, DEC_SEQ, D_MODEL), f32),
        "ffn_w_gate": jax.random.normal(ks[2], (DEPTH, N_FFN_PER_LAYER, D_MODEL, D_FF), f32) * d_in,
        "ffn_w_up": jax.random.normal(ks[3], (DEPTH, N_FFN_PER_LAYER, D_MODEL, D_FF), f32) * d_in,
        "ffn_w_down": jax.random.normal(ks[4], (DEPTH, N_FFN_PER_LAYER, D_FF, D_MODEL), f32) * (D_FF ** -0.5 * BETA),
        "ln_g": 1.0 + 0.02 * jax.random.normal(ks[5], (DEPTH, N_LN_PER_LAYER, D_MODEL), f32),
        "ln_b": 0.02 * jax.random.normal(ks[6], (DEPTH, N_LN_PER_LAYER, D_MODEL), f32),
        "fnet_w_in": jax.random.normal(ks[7], (N_FNET_LAYERS, D_MODEL, D_MIX), f32) * d_in,
        "fnet_w_out": jax.random.normal(ks[8], (N_FNET_LAYERS, D_MIX, D_MODEL), f32) * (D_MIX ** -0.5 * BETA),
        "conv_w_in": jax.random.normal(ks[9], (N_CONV_LAYERS, D_MODEL, 3 * D_MIX), f32) * d_in,
        "conv_w": jax.random.normal(ks[10], (N_CONV_LAYERS, CONV_WIDTH, D_MIX), f32) * (CONV_WIDTH ** -0.5),
        "conv_w_out": jax.random.normal(ks[11], (N_CONV_LAYERS, D_MIX, D_MODEL), f32) * (D_MIX ** -0.5 * BETA),
    }


def reference(x_prompt, x_sample, ffn_w_gate, ffn_w_up, ffn_w_down, ln_g, ln_b,
              fnet_w_in, fnet_w_out, conv_w_in, conv_w, conv_w_out):
    y_prompt = trunk(x_prompt, ffn_w_gate, ffn_w_up, ffn_w_down, ln_g, ln_b,
                     fnet_w_in, fnet_w_out, conv_w_in, conv_w, conv_w_out)
    y_sample = trunk(x_sample, ffn_w_gate, ffn_w_up, ffn_w_down, ln_g, ln_b,
                     fnet_w_in, fnet_w_out, conv_w_in, conv_w, conv_w_out)
    return (y_prompt, y_sample)
```

```python
import functools

import numpy as np
import jax
import jax.numpy as jnp
from jax import lax
from jax.experimental import pallas as pl
from jax.experimental.pallas import tpu as pltpu

D_MODEL = 1024
D_FF = 2816
DEPTH = 2
FNET_GROUPS = 8
FNET_GROUP_DIM = D_MODEL // FNET_GROUPS
ALPHA = float((2 * DEPTH) ** 0.25)
LN_EPS = 1e-5

F32 = jnp.float32
BF16 = jnp.bfloat16

V7X_VMEM_BYTES = 64 * 1024 * 1024
VMEM_LIMIT_BYTES = V7X_VMEM_BYTES - 8 * 1024 * 1024
SUBLANES = 8

DFT_N1 = 128
FFN_TOKENS = 512
CONV_TOKENS = 512
FNET_A_COLS = 4
FNET_B_TOKENS = 1024


def _layer_norm(z, g, b):
    mu = jnp.mean(z, axis=-1, keepdims=True)
    zc = z - mu
    var = jnp.mean(zc * zc, axis=-1, keepdims=True)
    return zc * lax.rsqrt(var + LN_EPS) * g + b


def _dot(a, b):
    return jnp.dot(a, b, preferred_element_type=F32)


def _resident(shape):
    return pl.BlockSpec(shape, lambda *_: (0,) * len(shape), pipeline_mode=pl.Buffered(1))


def _params(n_axes):
    return pltpu.CompilerParams(
        dimension_semantics=("arbitrary",) * n_axes,
        vmem_limit_bytes=VMEM_LIMIT_BYTES,
    )


def _ffn_kernel(x_ref, wg_ref, wu_ref, wd_ref, g_ref, b_ref, o_ref):
    x = x_ref[...]
    xb = x.astype(BF16)
    gate = _dot(xb, wg_ref[...])
    up = _dot(xb, wu_ref[...])
    h = (gate * jax.nn.sigmoid(gate) * up).astype(BF16)
    y = _dot(h, wd_ref[...])
    o_ref[...] = _layer_norm(ALPHA * x + 0.5 * y, g_ref[...], b_ref[...])


def _ffn_ln(x2d, wg, wu, wd, g, b):
    n, d = x2d.shape
    tm = FFN_TOKENS
    assert n % tm == 0
    row = pl.BlockSpec((tm, d), lambda i: (i, 0))
    return pl.pallas_call(
        _ffn_kernel,
        grid=(n // tm,),
        in_specs=[row, _resident(wg.shape), _resident(wu.shape), _resident(wd.shape),
                  _resident(g.shape), _resident(b.shape)],
        out_specs=row,
        out_shape=jax.ShapeDtypeStruct((n, d), F32),
        compiler_params=_params(1),
        name="ffn_ln",
    )(x2d, wg, wu, wd, g, b)


def _conv_kernel(x_ref, xp_ref, xn_ref, win_ref, cw_ref, wout_ref, g_ref, b_ref, o_ref,
                 *, tiles_per_seq):
    tm, d = x_ref.shape
    i = pl.program_id(0)
    first = (i % tiles_per_seq) == 0
    last = (i % tiles_per_seq) == tiles_per_seq - 1
    x = x_ref[...]
    xp = jnp.where(first, 0.0, xp_ref[...])
    xn = jnp.where(last, 0.0, xn_ref[...])
    xa = jnp.concatenate([xp, x, xn], axis=0).astype(BF16)
    gate_b = _dot(xa[SUBLANES:SUBLANES + tm], win_ref[:, :d])
    gate_c = _dot(xa, win_ref[:, d:2 * d])
    hval = _dot(xa, win_ref[:, 2 * d:])
    v = gate_c * hval
    rows = tm + 2 * SUBLANES
    v_prev = pltpu.roll(v, 1, axis=0)[SUBLANES:SUBLANES + tm]
    v_next = pltpu.roll(v, rows - 1, axis=0)[SUBLANES:SUBLANES + tm]
    v_mid = v[SUBLANES:SUBLANES + tm]
    cw = cw_ref[...]
    conv = v_prev * cw[0:1] + v_mid * cw[1:2] + v_next * cw[2:3]
    y = _dot((gate_b * conv).astype(BF16), wout_ref[...])
    o_ref[...] = _layer_norm(ALPHA * x + y, g_ref[...], b_ref[...])


def _conv_ln(x2d, seq_len, win, cw, wout, g, b):
    n, d = x2d.shape
    tm = CONV_TOKENS
    assert seq_len % tm == 0 and tm % SUBLANES == 0
    halo_per_tile = tm // SUBLANES
    n_halo_blocks = n // SUBLANES
    row = pl.BlockSpec((tm, d), lambda i: (i, 0))
    prev = pl.BlockSpec((SUBLANES, d), lambda i: (jnp.maximum(i * halo_per_tile - 1, 0), 0))
    nxt = pl.BlockSpec((SUBLANES, d),
                       lambda i: (jnp.minimum((i + 1) * halo_per_tile, n_halo_blocks - 1), 0))
    return pl.pallas_call(
        functools.partial(_conv_kernel, tiles_per_seq=seq_len // tm),
        grid=(n // tm,),
        in_specs=[row, prev, nxt, _resident(win.shape), _resident(cw.shape),
                  _resident(wout.shape), _resident(g.shape), _resident(b.shape)],
        out_specs=row,
        out_shape=jax.ShapeDtypeStruct((n, d), F32),
        compiler_params=_params(1),
        name="conv_ln",
    )(x2d, x2d, x2d, win, cw, wout, g, b)


def _dft_cos_sin(n):
    idx = np.arange(n)
    ang = 2.0 * np.pi * ((idx[:, None] * idx[None, :]) % n) / n
    return np.cos(ang), np.sin(ang)


def _fnet_constants(seq_len):
    n1 = DFT_N1
    n2 = seq_len // n1
    cg, sg = _dft_cos_sin(FNET_GROUP_DIM)
    gcs = np.concatenate([cg, sg], axis=1) / np.sqrt(FNET_GROUP_DIM)
    c1, s1 = _dft_cos_sin(n1)
    m1 = np.block([[c1, -s1], [s1, c1]]) / np.sqrt(n1)
    c2, s2 = _dft_cos_sin(n2)
    m2 = np.concatenate([c2, -s2], axis=1) / np.sqrt(n2)
    ang = 2.0 * np.pi * ((np.arange(n2)[:, None] * np.arange(n1)[None, :]) % seq_len) / seq_len
    twc = jnp.broadcast_to(jnp.asarray(np.cos(ang), F32)[:, :, None], (n2, n1, 128))
    tws = jnp.broadcast_to(jnp.asarray(np.sin(ang), F32)[:, :, None], (n2, n1, 128))
    return (jnp.asarray(gcs, F32).astype(BF16), jnp.asarray(m1, F32).astype(BF16),
            jnp.asarray(m2, F32).astype(BF16), twc, tws)


def _fnet_a_kernel(x_ref, win_ref, gcs_ref, m1_ref, twc_ref, tws_ref, tr_ref, tm_ref, *, cols):
    n1 = x_ref.shape[0]
    d = D_MODEL
    gd = FNET_GROUP_DIM
    xs = jnp.concatenate([x_ref[:, j * d:(j + 1) * d] for j in range(cols)], axis=0).astype(BF16)
    u = _dot(xs, win_ref[...]).astype(BF16)
    gcs = gcs_ref[...]
    pq = [_dot(u[:, g * gd:(g + 1) * gd], gcs) for g in range(FNET_GROUPS)]
    p = jnp.concatenate([t[:, :gd] for t in pq], axis=1).astype(BF16)
    q = jnp.concatenate([t[:, gd:] for t in pq], axis=1).astype(BF16)
    m1 = m1_ref[...]
    for j in range(cols):
        rhs = jnp.concatenate([p[j * n1:(j + 1) * n1], q[j * n1:(j + 1) * n1]], axis=0)
        t = _dot(m1, rhs)
        t_re, t_im = t[:n1], t[n1:]
        c = jnp.concatenate([twc_ref[j]] * (d // 128), axis=1)
        s = jnp.concatenate([tws_ref[j]] * (d // 128), axis=1)
        tr_ref[:, j * d:(j + 1) * d] = (t_re * c - t_im * s).astype(BF16)
        tm_ref[:, j * d:(j + 1) * d] = (t_re * s + t_im * c).astype(BF16)


def _fnet_b_kernel(tr_ref, tm_ref, x_ref, m2_ref, wout_ref, g_ref, b_ref, o_ref, *, cols):
    n2 = x_ref.shape[0]
    d = D_MODEL
    m2 = m2_ref[...]
    fs = []
    for k in range(cols):
        rhs = jnp.concatenate([tr_ref[k * n2:(k + 1) * n2, :], tm_ref[k * n2:(k + 1) * n2, :]], axis=0)
        fs.append(_dot(m2, rhs))
    f = jnp.concatenate(fs, axis=0).astype(BF16)
    m = _dot(f, wout_ref[...])
    g = g_ref[...]
    b = b_ref[...]
    for k in range(cols):
        z = ALPHA * x_ref[:, k * d:(k + 1) * d] + m[k * n2:(k + 1) * n2]
        o_ref[:, k * d:(k + 1) * d] = _layer_norm(z, g, b)


def _fnet_ln(x3d, win, wout, g, b):
    bsz, s, d = x3d.shape
    n1 = DFT_N1
    n2 = s // n1
    assert n1 * n2 == s and d == D_MODEL
    gcs, m1, m2, twc, tws = _fnet_constants(s)

    ca = FNET_A_COLS
    assert n2 % ca == 0
    xa = x3d.reshape(bsz, n1, n2 * d)
    a_blk = pl.BlockSpec((None, n1, ca * d), lambda bi, j: (bi, 0, j))
    tw_blk = pl.BlockSpec((ca, n1, 128), lambda bi, j: (j, 0, 0))
    t_shape = jax.ShapeDtypeStruct((bsz, n1, n2 * d), BF16)
    t_re, t_im = pl.pallas_call(
        functools.partial(_fnet_a_kernel, cols=ca),
        grid=(bsz, n2 // ca),
        in_specs=[a_blk, _resident(win.shape), _resident(gcs.shape), _resident(m1.shape),
                  tw_blk, tw_blk],
        out_specs=[a_blk, a_blk],
        out_shape=[t_shape, t_shape],
        compiler_params=_params(2),
        name="fnet_stage1",
    )(xa, win, gcs, m1, twc, tws)

    cb = FNET_B_TOKENS // n2
    assert n1 % cb == 0
    t_re = t_re.reshape(bsz, n1 * n2, d)
    t_im = t_im.reshape(bsz, n1 * n2, d)
    xb = x3d.reshape(bsz, n2, n1 * d)
    t_blk = pl.BlockSpec((None, cb * n2, d), lambda bi, k: (bi, k, 0))
    b_blk = pl.BlockSpec((None, n2, cb * d), lambda bi, k: (bi, 0, k))
    out = pl.pallas_call(
        functools.partial(_fnet_b_kernel, cols=cb),
        grid=(bsz, n1 // cb),
        in_specs=[t_blk, t_blk, b_blk, _resident(m2.shape), _resident(wout.shape),
                  _resident(g.shape), _resident(b.shape)],
        out_specs=b_blk,
        out_shape=jax.ShapeDtypeStruct((bsz, n2, n1 * d), F32),
        compiler_params=_params(2),
        name="fnet_stage2",
    )(t_re, t_im, xb, m2, wout, g, b)
    return out.reshape(bsz, s, d)


def _trunk(x, w):
    bsz, s, d = x.shape
    for i in range(DEPTH):
        j = i // 2
        x = _ffn_ln(x.reshape(bsz * s, d), w["wg"][i][0], w["wu"][i][0], w["wd"][i][0],
                    w["ln_g"][i][0], w["ln_b"][i][0]).reshape(bsz, s, d)
        if i % 2 == 0:
            x = _fnet_ln(x, w["fnet_in"][j], w["fnet_out"][j], w["ln_g"][i][1], w["ln_b"][i][1])
        else:
            x = _conv_ln(x.reshape(bsz * s, d), s, w["conv_in"][j], w["conv_w"][j], w["conv_out"][j],
                         w["ln_g"][i][1], w["ln_b"][i][1]).reshape(bsz, s, d)
        x = _ffn_ln(x.reshape(bsz * s, d), w["wg"][i][1], w["wu"][i][1], w["wd"][i][1],
                    w["ln_g"][i][2], w["ln_b"][i][2]).reshape(bsz, s, d)
    return x


def kernel(x_prompt, x_sample, ffn_w_gate, ffn_w_up, ffn_w_down, ln_g, ln_b,
           fnet_w_in, fnet_w_out, conv_w_in, conv_w, conv_w_out):
    n_layers, n_ln, d = ln_g.shape
    w = {
        "wg": ffn_w_gate.astype(BF16),
        "wu": ffn_w_up.astype(BF16),
        "wd": ffn_w_down.astype(BF16),
        "ln_g": ln_g.reshape(n_layers, n_ln, 1, d),
        "ln_b": ln_b.reshape(n_layers, n_ln, 1, d),
        "fnet_in": fnet_w_in.astype(BF16),
        "fnet_out": fnet_w_out.astype(BF16),
        "conv_in": conv_w_in.astype(BF16),
        "conv_w": conv_w,
        "conv_out": conv_w_out.astype(BF16),
    }
    return (_trunk(x_prompt, w), _trunk(x_sample, w))
```

```python
import functools

import numpy as np
import jax
import jax.numpy as jnp
from jax import lax
from jax.experimental import pallas as pl
from jax.experimental.pallas import tpu as pltpu

D_MODEL = 1024
D_FF = 2816
DEPTH = 2
FNET_GROUPS = 8
FNET_GROUP_DIM = D_MODEL // FNET_GROUPS
ALPHA = float((2 * DEPTH) ** 0.25)
LN_EPS = 1e-5

F32 = jnp.float32
BF16 = jnp.bfloat16

V7X_VMEM_BYTES = 64 * 1024 * 1024
VMEM_LIMIT_BYTES = V7X_VMEM_BYTES - 8 * 1024 * 1024
SUBLANES = 8
LANES = 128

DFT_N2 = 128
FFN_TOKENS = SUBLANES * DFT_N2
FFN_SUB_TOKENS = 256
CONV_TOKENS = 512
FNET_TOKENS = 1024


def _layer_norm(z, g, b):
    mu = jnp.mean(z, axis=-1, keepdims=True)
    zc = z - mu
    var = jnp.mean(zc * zc, axis=-1, keepdims=True)
    return zc * lax.rsqrt(var + LN_EPS) * g + b


def _dot(a, b):
    return jnp.dot(a, b, preferred_element_type=F32)


def _resident(shape):
    return pl.BlockSpec(shape, lambda *_: (0,) * len(shape), pipeline_mode=pl.Buffered(1))


def _params(n_axes):
    return pltpu.CompilerParams(
        dimension_semantics=("arbitrary",) * n_axes,
        vmem_limit_bytes=VMEM_LIMIT_BYTES,
    )


def _ffn_kernel(x_ref, wg_ref, wu_ref, wd_ref, g_ref, b_ref, o_ref, *, swapped_in, swapped_out):
    n2, sub = DFT_N2, FFN_SUB_TOKENS
    per_sub = sub // n2
    wg, wu, wd = wg_ref[...], wu_ref[...], wd_ref[...]
    g, b = g_ref[...], b_ref[...]
    for s in range(FFN_TOKENS // sub):
        if swapped_in:
            x = jnp.concatenate([x_ref[:, s * per_sub + r, :] for r in range(per_sub)], axis=0)
        else:
            x = x_ref[s * sub:(s + 1) * sub, :]
        xb = x.astype(BF16)
        gate = _dot(xb, wg)
        up = _dot(xb, wu)
        h = (gate * jax.nn.sigmoid(gate) * up).astype(BF16)
        y = _layer_norm(ALPHA * x + 0.5 * _dot(h, wd), g, b)
        if swapped_out:
            for r in range(per_sub):
                o_ref[:, s * per_sub + r, :] = y[r * n2:(r + 1) * n2]
        else:
            o_ref[s * sub:(s + 1) * sub, :] = y


def _ffn_ln(x, wg, wu, wd, g, b, *, bsz, seq_len, swapped_in=False, swapped_out=False):
    d = D_MODEL
    n = bsz * seq_len
    tm = FFN_TOKENS
    n1 = seq_len // DFT_N2
    tiles_per_seq = seq_len // tm
    assert seq_len % tm == 0 and n1 == tiles_per_seq * SUBLANES
    nat = pl.BlockSpec((tm, d), lambda i: (i, 0))
    swp = pl.BlockSpec((None, DFT_N2, SUBLANES, d),
                       lambda i: (i // tiles_per_seq, 0, i % tiles_per_seq, 0))
    nat_shape = jax.ShapeDtypeStruct((n, d), F32)
    swp_shape = jax.ShapeDtypeStruct((bsz, DFT_N2, n1, d), F32)
    return pl.pallas_call(
        functools.partial(_ffn_kernel, swapped_in=swapped_in, swapped_out=swapped_out),
        grid=(n // tm,),
        in_specs=[swp if swapped_in else nat, _resident(wg.shape), _resident(wu.shape),
                  _resident(wd.shape), _resident(g.shape), _resident(b.shape)],
        out_specs=swp if swapped_out else nat,
        out_shape=swp_shape if swapped_out else nat_shape,
        compiler_params=_params(1),
        name="ffn_ln",
    )(x, wg, wu, wd, g, b)


def _conv_kernel(x_ref, xp_ref, xn_ref, win_ref, cw_ref, wout_ref, g_ref, b_ref, o_ref,
                 *, tiles_per_seq):
    tm, d = x_ref.shape
    i = pl.program_id(0)
    first = (i % tiles_per_seq) == 0
    last = (i % tiles_per_seq) == tiles_per_seq - 1
    x = x_ref[...]
    xp = jnp.where(first, 0.0, xp_ref[...])
    xn = jnp.where(last, 0.0, xn_ref[...])
    xa = jnp.concatenate([xp, x, xn], axis=0).astype(BF16)
    gate_b = _dot(xa[SUBLANES:SUBLANES + tm], win_ref[:, :d])
    gate_c = _dot(xa, win_ref[:, d:2 * d])
    hval = _dot(xa, win_ref[:, 2 * d:])
    v = gate_c * hval
    rows = tm + 2 * SUBLANES
    v_prev = pltpu.roll(v, 1, axis=0)[SUBLANES:SUBLANES + tm]
    v_next = pltpu.roll(v, rows - 1, axis=0)[SUBLANES:SUBLANES + tm]
    v_mid = v[SUBLANES:SUBLANES + tm]
    cw = cw_ref[...]
    conv = v_prev * cw[0:1] + v_mid * cw[1:2] + v_next * cw[2:3]
    y = _dot((gate_b * conv).astype(BF16), wout_ref[...])
    o_ref[...] = _layer_norm(ALPHA * x + y, g_ref[...], b_ref[...])


def _conv_ln(x2d, seq_len, win, cw, wout, g, b):
    n, d = x2d.shape
    tm = CONV_TOKENS
    assert seq_len % tm == 0 and tm % SUBLANES == 0
    halo_per_tile = tm // SUBLANES
    n_halo_blocks = n // SUBLANES
    row = pl.BlockSpec((tm, d), lambda i: (i, 0))
    prev = pl.BlockSpec((SUBLANES, d), lambda i: (jnp.maximum(i * halo_per_tile - 1, 0), 0))
    nxt = pl.BlockSpec((SUBLANES, d),
                       lambda i: (jnp.minimum((i + 1) * halo_per_tile, n_halo_blocks - 1), 0))
    return pl.pallas_call(
        functools.partial(_conv_kernel, tiles_per_seq=seq_len // tm),
        grid=(n // tm,),
        in_specs=[row, prev, nxt, _resident(win.shape), _resident(cw.shape),
                  _resident(wout.shape), _resident(g.shape), _resident(b.shape)],
        out_specs=row,
        out_shape=jax.ShapeDtypeStruct((n, d), F32),
        compiler_params=_params(1),
        name="conv_ln",
    )(x2d, x2d, x2d, win, cw, wout, g, b)


def _dft_cos_sin(rows, cols, n):
    ang = 2.0 * np.pi * ((rows[:, None] * cols[None, :]) % n) / n
    return np.cos(ang), np.sin(ang)


def _fnet_constants(seq_len):
    n2 = DFT_N2
    n1 = seq_len // n2
    m = n2 // n1
    gd = np.arange(FNET_GROUP_DIM)
    cg, sg = _dft_cos_sin(gd, gd, FNET_GROUP_DIM)
    gcs = np.concatenate([cg, sg], axis=1) / np.sqrt(FNET_GROUP_DIM)
    i1 = np.arange(n1)
    c1, s1 = _dft_cos_sin(i1, i1, n1)
    m1 = np.block([[c1, -s1], [s1, c1]]) / np.sqrt(n1)
    k2 = (m * np.arange(n1)[None, :] + np.arange(m)[:, None]).reshape(-1)
    c2, s2 = _dft_cos_sin(k2, np.arange(n2), n2)
    m2 = np.concatenate([c2, -s2], axis=1) / np.sqrt(n2)
    ang = 2.0 * np.pi * ((np.arange(n2)[:, None] * i1[None, :]) % seq_len) / seq_len
    twc = jnp.broadcast_to(jnp.asarray(np.cos(ang), F32)[:, :, None], (n2, n1, LANES))
    tws = jnp.broadcast_to(jnp.asarray(np.sin(ang), F32)[:, :, None], (n2, n1, LANES))
    return (jnp.asarray(gcs, F32).astype(BF16), jnp.asarray(m1, F32).astype(BF16),
            jnp.asarray(m2, F32).astype(BF16), twc, tws)


def _fnet_stage1_kernel(x_ref, win_ref, gcs_ref, m1_ref, twc_ref, tws_ref, tr_ref, ti_ref):
    cols, n1, d = x_ref.shape
    gd = FNET_GROUP_DIM
    xs = x_ref[...].reshape(cols * n1, d).astype(BF16)
    u = _dot(xs, win_ref[...]).astype(BF16)
    gcs = gcs_ref[...]
    pq = [_dot(u[:, g * gd:(g + 1) * gd], gcs) for g in range(FNET_GROUPS)]
    p = jnp.concatenate([t[:, :gd] for t in pq], axis=1).astype(BF16)
    q = jnp.concatenate([t[:, gd:] for t in pq], axis=1).astype(BF16)
    m1 = m1_ref[...]
    for j in range(cols):
        rhs = jnp.concatenate([p[j * n1:(j + 1) * n1], q[j * n1:(j + 1) * n1]], axis=0)
        t = _dot(m1, rhs)
        t_re, t_im = t[:n1], t[n1:]
        c = jnp.concatenate([twc_ref[j]] * (d // LANES), axis=1)
        s = jnp.concatenate([tws_ref[j]] * (d // LANES), axis=1)
        tr_ref[:, j, :] = t_re * c - t_im * s
        ti_ref[:, j, :] = t_re * s + t_im * c


def _fnet_stage2_kernel(tr_ref, ti_ref, x_ref, m2_ref, wout_ref, g_ref, b_ref, o_ref):
    m, cols, n1, d = x_ref.shape
    m2 = m2_ref[...]
    fs = []
    for k in range(cols):
        rhs = jnp.concatenate([tr_ref[k].astype(BF16), ti_ref[k].astype(BF16)], axis=0)
        fs.append(_dot(m2, rhs))
    f = jnp.concatenate(fs, axis=0).astype(BF16)
    mix = _dot(f, wout_ref[...])
    g = g_ref[...]
    b = b_ref[...]
    n2 = m * n1
    for k in range(cols):
        for h in range(m):
            z = ALPHA * x_ref[h, k] + mix[k * n2 + h * n1:k * n2 + (h + 1) * n1]
            o_ref[h, k] = _layer_norm(z, g, b)


def _fnet_ln(xs4, win, wout, g, b):
    bsz, n2, n1, d = xs4.shape
    assert n2 == DFT_N2 and n2 % n1 == 0 and d == D_MODEL
    m = n2 // n1
    gcs, m1, m2, twc, tws = _fnet_constants(n1 * n2)

    ca = FNET_TOKENS // n1
    assert n2 % ca == 0
    x_blk = pl.BlockSpec((None, ca, n1, d), lambda bi, j: (bi, j, 0, 0))
    tw_blk = pl.BlockSpec((ca, n1, LANES), lambda bi, j: (j, 0, 0))
    t_out = pl.BlockSpec((None, n1, ca, d), lambda bi, j: (bi, 0, j, 0))
    t_shape = jax.ShapeDtypeStruct((bsz, n1, n2, d), F32)
    t_re, t_im = pl.pallas_call(
        _fnet_stage1_kernel,
        grid=(bsz, n2 // ca),
        in_specs=[x_blk, _resident(win.shape), _resident(gcs.shape), _resident(m1.shape),
                  tw_blk, tw_blk],
        out_specs=[t_out, t_out],
        out_shape=[t_shape, t_shape],
        compiler_params=_params(2),
        name="fnet_stage1",
    )(xs4, win, gcs, m1, twc, tws)

    cb = FNET_TOKENS // n2
    assert n1 % cb == 0
    xs5 = xs4.reshape(bsz, m, n1, n1, d)
    t_in = pl.BlockSpec((None, cb, n2, d), lambda bi, k: (bi, k, 0, 0))
    r_blk = pl.BlockSpec((None, m, cb, n1, d), lambda bi, k: (bi, 0, k, 0, 0))
    out = pl.pallas_call(
        _fnet_stage2_kernel,
        grid=(bsz, n1 // cb),
        in_specs=[t_in, t_in, r_blk, _resident(m2.shape), _resident(wout.shape),
                  _resident(g.shape), _resident(b.shape)],
        out_specs=r_blk,
        out_shape=jax.ShapeDtypeStruct((bsz, m, n1, n1, d), F32),
        compiler_params=_params(2),
        name="fnet_stage2",
    )(t_re, t_im, xs5, m2, wout, g, b)
    return out.reshape(bsz, n2, n1, d)


def _trunk(x, w):
    bsz, s, d = x.shape
    assert DEPTH == 2 and d == D_MODEL
    ffn = functools.partial(_ffn_ln, bsz=bsz, seq_len=s)

    def ffn_w(i, k):
        return (w["wg"][i][k], w["wu"][i][k], w["wd"][i][k], w["ln_g"][i][2 * k], w["ln_b"][i][2 * k])

    xs = ffn(x.reshape(bsz * s, d), *ffn_w(0, 0), swapped_out=True)
    xs = _fnet_ln(xs, w["fnet_in"][0], w["fnet_out"][0], w["ln_g"][0][1], w["ln_b"][0][1])
    xs = ffn(xs.reshape(bsz * s, d), *ffn_w(0, 1))
    x2 = ffn(xs.reshape(bsz, DFT_N2, s // DFT_N2, d), *ffn_w(1, 0), swapped_in=True)
    x2 = _conv_ln(x2, s, w["conv_in"][0], w["conv_w"][0], w["conv_out"][0],
                  w["ln_g"][1][1], w["ln_b"][1][1])
    x2 = ffn(x2, *ffn_w(1, 1))
    return x2.reshape(bsz, s, d)


def kernel(x_prompt, x_sample, ffn_w_gate, ffn_w_up, ffn_w_down, ln_g, ln_b,
           fnet_w_in, fnet_w_out, conv_w_in, conv_w, conv_w_out):
    n_layers, n_ln, d = ln_g.shape
    w = {
        "wg": ffn_w_gate.astype(BF16),
        "wu": ffn_w_up.astype(BF16),
        "wd": ffn_w_down.astype(BF16),
        "ln_g": ln_g.reshape(n_layers, n_ln, 1, d),
        "ln_b": ln_b.reshape(n_layers, n_ln, 1, d),
        "fnet_in": fnet_w_in.astype(BF16),
        "fnet_out": fnet_w_out.astype(BF16),
        "conv_in": conv_w_in.astype(BF16),
        "conv_w": conv_w,
        "conv_out": conv_w_out.astype(BF16),
    }
    return (_trunk(x_prompt, w), _trunk(x_sample, w))
```

```python
import functools

import numpy as np
import jax
import jax.numpy as jnp
from jax import lax
from jax.experimental import pallas as pl
from jax.experimental.pallas import tpu as pltpu

D_MODEL = 1024
D_FF = 2816
DEPTH = 2
FNET_GROUPS = 8
FNET_GROUP_DIM = D_MODEL // FNET_GROUPS
ALPHA = float((2 * DEPTH) ** 0.25)
LN_EPS = 1e-5

F32 = jnp.float32
BF16 = jnp.bfloat16

V7X_VMEM_BYTES = 64 * 1024 * 1024
VMEM_LIMIT_BYTES = V7X_VMEM_BYTES - 8 * 1024 * 1024
SUBLANES = 8
LANES = 128

DFT_N2 = 128
FFN_TOKENS = SUBLANES * DFT_N2
FFN_SUB_TOKENS = 256
CONV_TOKENS = 1024
CONV_SUB_TOKENS = 256
FNET_TOKENS = 1024
FNET_SUB_TOKENS = 256


def _layer_norm(z, g, b):
    mu = jnp.mean(z, axis=-1, keepdims=True)
    zc = z - mu
    var = jnp.mean(zc * zc, axis=-1, keepdims=True)
    return zc * lax.rsqrt(var + LN_EPS) * g + b


def _dot(a, b):
    return jnp.dot(a, b, preferred_element_type=F32)


def _resident(arr, *lead):
    rest = arr.shape[len(lead):]
    index = tuple(lead) + (0,) * len(rest)
    return pl.BlockSpec((None,) * len(lead) + rest, lambda *_: index, pipeline_mode=pl.Buffered(1))


def _params(n_axes):
    return pltpu.CompilerParams(
        dimension_semantics=("arbitrary",) * n_axes,
        vmem_limit_bytes=VMEM_LIMIT_BYTES,
    )


def _ffn_kernel(x_ref, wg_ref, wu_ref, wd_ref, g_ref, b_ref, o_ref, *, swapped_out):
    n2, sub = DFT_N2, FFN_SUB_TOKENS
    per_sub = sub // n2
    wg, wu, wd = wg_ref[...], wu_ref[...], wd_ref[...]
    g, b = g_ref[...], b_ref[...]
    for s in range(FFN_TOKENS // sub):
        x = x_ref[s * sub:(s + 1) * sub, :]
        xb = x.astype(BF16)
        gate = _dot(xb, wg)
        up = _dot(xb, wu)
        h = (gate * jax.nn.sigmoid(gate) * up).astype(BF16)
        y = _layer_norm(ALPHA * x + 0.5 * _dot(h, wd), g, b)
        if swapped_out:
            for r in range(per_sub):
                o_ref[:, s * per_sub + r, :] = y[r * n2:(r + 1) * n2]
        else:
            o_ref[s * sub:(s + 1) * sub, :] = y


def _ffn_ln(x2d, w, layer, half, *, bsz, seq_len, swapped_out=False):
    d = D_MODEL
    n = bsz * seq_len
    tm = FFN_TOKENS
    n1 = seq_len // DFT_N2
    tiles_per_seq = seq_len // tm
    assert seq_len % tm == 0 and n1 == tiles_per_seq * SUBLANES
    nat = pl.BlockSpec((tm, d), lambda i: (i, 0))
    swp = pl.BlockSpec((None, DFT_N2, SUBLANES, d),
                       lambda i: (i // tiles_per_seq, 0, i % tiles_per_seq, 0))
    nat_shape = jax.ShapeDtypeStruct((n, d), F32)
    swp_shape = jax.ShapeDtypeStruct((bsz, DFT_N2, n1, d), F32)
    ln = 2 * half
    return pl.pallas_call(
        functools.partial(_ffn_kernel, swapped_out=swapped_out),
        grid=(n // tm,),
        in_specs=[nat, _resident(w["wg"], layer, half), _resident(w["wu"], layer, half),
                  _resident(w["wd"], layer, half), _resident(w["ln_g"], layer, ln),
                  _resident(w["ln_b"], layer, ln)],
        out_specs=swp if swapped_out else nat,
        out_shape=swp_shape if swapped_out else nat_shape,
        compiler_params=_params(1),
        name="ffn_ln",
    )(x2d, w["wg"], w["wu"], w["wd"], w["ln_g"], w["ln_b"])


def _conv_kernel(x_ref, xp_ref, xn_ref, win_ref, cw_ref, wout_ref, g_ref, b_ref, o_ref,
                 *, tiles_per_seq):
    tm, d = x_ref.shape
    sub = CONV_SUB_TOKENS
    n_sub = tm // sub
    i = pl.program_id(0)
    first = (i % tiles_per_seq) == 0
    last = (i % tiles_per_seq) == tiles_per_seq - 1
    xp = jnp.where(first, 0.0, xp_ref[...])
    xn = jnp.where(last, 0.0, xn_ref[...])
    w_b, w_c, w_h = win_ref[:, :d], win_ref[:, d:2 * d], win_ref[:, 2 * d:]
    wout, cw, g, b = wout_ref[...], cw_ref[...], g_ref[...], b_ref[...]
    rows = sub + 2 * SUBLANES
    for s in range(n_sub):
        lo, hi = s * sub, (s + 1) * sub
        x = x_ref[lo:hi, :]
        before = xp if s == 0 else x_ref[lo - SUBLANES:lo, :]
        after = xn if s == n_sub - 1 else x_ref[hi:hi + SUBLANES, :]
        xa = jnp.concatenate([before, x, after], axis=0).astype(BF16)
        gate_b = _dot(xa[SUBLANES:SUBLANES + sub], w_b)
        v = _dot(xa, w_c) * _dot(xa, w_h)
        v_prev = pltpu.roll(v, 1, axis=0)[SUBLANES:SUBLANES + sub]
        v_next = pltpu.roll(v, rows - 1, axis=0)[SUBLANES:SUBLANES + sub]
        v_mid = v[SUBLANES:SUBLANES + sub]
        conv = v_prev * cw[0:1] + v_mid * cw[1:2] + v_next * cw[2:3]
        y = _dot((gate_b * conv).astype(BF16), wout)
        o_ref[lo:hi, :] = _layer_norm(ALPHA * x + y, g, b)


def _conv_ln(x2d, seq_len, w, layer, j):
    n, d = x2d.shape
    tm = CONV_TOKENS
    assert seq_len % tm == 0 and tm % CONV_SUB_TOKENS == 0
    halo_per_tile = tm // SUBLANES
    n_halo_blocks = n // SUBLANES
    row = pl.BlockSpec((tm, d), lambda i: (i, 0))
    prev = pl.BlockSpec((SUBLANES, d), lambda i: (jnp.maximum(i * halo_per_tile - 1, 0), 0))
    nxt = pl.BlockSpec((SUBLANES, d),
                       lambda i: (jnp.minimum((i + 1) * halo_per_tile, n_halo_blocks - 1), 0))
    return pl.pallas_call(
        functools.partial(_conv_kernel, tiles_per_seq=seq_len // tm),
        grid=(n // tm,),
        in_specs=[row, prev, nxt, _resident(w["conv_in"], j), _resident(w["conv_w"], j),
                  _resident(w["conv_out"], j), _resident(w["ln_g"], layer, 1),
                  _resident(w["ln_b"], layer, 1)],
        out_specs=row,
        out_shape=jax.ShapeDtypeStruct((n, d), F32),
        compiler_params=_params(1),
        name="conv_ln",
    )(x2d, x2d, x2d, w["conv_in"], w["conv_w"], w["conv_out"], w["ln_g"], w["ln_b"])


def _dft_cos_sin(rows, cols, n):
    ang = 2.0 * np.pi * ((rows[:, None] * cols[None, :]) % n) / n
    return np.cos(ang), np.sin(ang)


def _fnet_constants(seq_len):
    n2 = DFT_N2
    n1 = seq_len // n2
    m = n2 // n1
    gd = np.arange(FNET_GROUP_DIM)
    cg, sg = _dft_cos_sin(gd, gd, FNET_GROUP_DIM)
    gcs = np.concatenate([cg, sg], axis=1) / np.sqrt(FNET_GROUP_DIM)
    i1 = np.arange(n1)
    c1, s1 = _dft_cos_sin(i1, i1, n1)
    m1 = np.block([[c1, -s1], [s1, c1]]) / np.sqrt(n1)
    k2 = (m * np.arange(n1)[None, :] + np.arange(m)[:, None]).reshape(-1)
    c2, s2 = _dft_cos_sin(k2, np.arange(n2), n2)
    m2 = np.concatenate([c2, -s2], axis=1) / np.sqrt(n2)
    ang = 2.0 * np.pi * ((np.arange(n2)[:, None] * i1[None, :]) % seq_len) / seq_len
    twc = jnp.broadcast_to(jnp.asarray(np.cos(ang), F32)[:, :, None], (n2, n1, LANES))
    tws = jnp.broadcast_to(jnp.asarray(np.sin(ang), F32)[:, :, None], (n2, n1, LANES))
    return (jnp.asarray(gcs, F32).astype(BF16), jnp.asarray(m1, F32).astype(BF16),
            jnp.asarray(m2, F32).astype(BF16), twc, tws)


def _fnet_stage1_kernel(x_ref, win_ref, gcs_ref, m1_ref, twc_ref, tws_ref, tr_ref, ti_ref):
    cols, n1, d = x_ref.shape
    gd = FNET_GROUP_DIM
    sub_cols = FNET_SUB_TOKENS // n1
    win, gcs, m1 = win_ref[...], gcs_ref[...], m1_ref[...]
    for c0 in range(0, cols, sub_cols):
        xs = x_ref[c0:c0 + sub_cols].reshape(sub_cols * n1, d).astype(BF16)
        u = _dot(xs, win).astype(BF16)
        pq = [_dot(u[:, g * gd:(g + 1) * gd], gcs) for g in range(FNET_GROUPS)]
        p = jnp.concatenate([t[:, :gd] for t in pq], axis=1).astype(BF16)
        q = jnp.concatenate([t[:, gd:] for t in pq], axis=1).astype(BF16)
        for j in range(sub_cols):
            rhs = jnp.concatenate([p[j * n1:(j + 1) * n1], q[j * n1:(j + 1) * n1]], axis=0)
            t = _dot(m1, rhs)
            t_re, t_im = t[:n1], t[n1:]
            c = jnp.concatenate([twc_ref[c0 + j]] * (d // LANES), axis=1)
            s = jnp.concatenate([tws_ref[c0 + j]] * (d // LANES), axis=1)
            tr_ref[:, c0 + j, :] = t_re * c - t_im * s
            ti_ref[:, c0 + j, :] = t_re * s + t_im * c


def _fnet_stage2_kernel(tr_ref, ti_ref, x_ref, m2_ref, wout_ref, g_ref, b_ref, o_ref):
    m, cols, n1, d = x_ref.shape
    m2 = m2_ref[...]
    fs = []
    for k in range(cols):
        rhs = jnp.concatenate([tr_ref[k].astype(BF16), ti_ref[k].astype(BF16)], axis=0)
        fs.append(_dot(m2, rhs))
    f = jnp.concatenate(fs, axis=0).astype(BF16)
    mix = _dot(f, wout_ref[...])
    g = g_ref[...]
    b = b_ref[...]
    n2 = m * n1
    for k in range(cols):
        for h in range(m):
            z = ALPHA * x_ref[h, k] + mix[k * n2 + h * n1:k * n2 + (h + 1) * n1]
            o_ref[:, h, k, :] = _layer_norm(z, g, b)


def _fnet_ln(xs4, w, layer, j):
    bsz, n2, n1, d = xs4.shape
    assert n2 == DFT_N2 and n2 % n1 == 0 and d == D_MODEL
    m = n2 // n1
    gcs, m1, m2, twc, tws = _fnet_constants(n1 * n2)

    ca = FNET_TOKENS // n1
    assert n2 % ca == 0 and FNET_SUB_TOKENS % n1 == 0
    x_blk = pl.BlockSpec((None, ca, n1, d), lambda bi, c: (bi, c, 0, 0))
    tw_blk = pl.BlockSpec((ca, n1, LANES), lambda bi, c: (c, 0, 0))
    t_out = pl.BlockSpec((None, n1, ca, d), lambda bi, c: (bi, 0, c, 0))
    t_shape = jax.ShapeDtypeStruct((bsz, n1, n2, d), F32)
    t_re, t_im = pl.pallas_call(
        _fnet_stage1_kernel,
        grid=(bsz, n2 // ca),
        in_specs=[x_blk, _resident(w["fnet_in"], j), _resident(gcs), _resident(m1), tw_blk, tw_blk],
        out_specs=[t_out, t_out],
        out_shape=[t_shape, t_shape],
        compiler_params=_params(2),
        name="fnet_stage1",
    )(xs4, w["fnet_in"], gcs, m1, twc, tws)

    cb = FNET_TOKENS // n2
    assert n1 % cb == 0 and cb % SUBLANES == 0
    xs5 = xs4.reshape(bsz, m, n1, n1, d)
    t_in = pl.BlockSpec((None, cb, n2, d), lambda bi, k: (bi, k, 0, 0))
    r_blk = pl.BlockSpec((None, m, cb, n1, d), lambda bi, k: (bi, 0, k, 0, 0))
    o_blk = pl.BlockSpec((None, n1, m, cb, d), lambda bi, k: (bi, 0, 0, k, 0))
    out = pl.pallas_call(
        _fnet_stage2_kernel,
        grid=(bsz, n1 // cb),
        in_specs=[t_in, t_in, r_blk, _resident(m2), _resident(w["fnet_out"], j),
                  _resident(w["ln_g"], layer, 1), _resident(w["ln_b"], layer, 1)],
        out_specs=o_blk,
        out_shape=jax.ShapeDtypeStruct((bsz, n1, m, n1, d), F32),
        compiler_params=_params(2),
        name="fnet_stage2",
    )(t_re, t_im, xs5, m2, w["fnet_out"], w["ln_g"], w["ln_b"])
    return out.reshape(bsz, n1 * n2, d)


def _trunk(x, w):
    bsz, s, d = x.shape
    assert DEPTH == 2 and d == D_MODEL
    ffn = functools.partial(_ffn_ln, bsz=bsz, seq_len=s)
    x = x.reshape(bsz * s, d)
    xs = ffn(x, w, 0, 0, swapped_out=True)
    x = _fnet_ln(xs, w, 0, 0).reshape(bsz * s, d)
    x = ffn(x, w, 0, 1)
    x = ffn(x, w, 1, 0)
    x = _conv_ln(x, s, w, 1, 0)
    x = ffn(x, w, 1, 1)
    return x.reshape(bsz, s, d)


def kernel(x_prompt, x_sample, ffn_w_gate, ffn_w_up, ffn_w_down, ln_g, ln_b,
           fnet_w_in, fnet_w_out, conv_w_in, conv_w, conv_w_out):
    n_layers, n_ln, d = ln_g.shape
    w = {
        "wg": ffn_w_gate.astype(BF16),
        "wu": ffn_w_up.astype(BF16),
        "wd": ffn_w_down.astype(BF16),
        "ln_g": ln_g.reshape(n_layers, n_ln, 1, d),
        "ln_b": ln_b.reshape(n_layers, n_ln, 1, d),
        "fnet_in": fnet_w_in.astype(BF16),
        "fnet_out": fnet_w_out.astype(BF16),
        "conv_in": conv_w_in.astype(BF16),
        "conv_w": conv_w,
        "conv_out": conv_w_out.astype(BF16),
    }
    return (_trunk(x_prompt, w), _trunk(x_sample, w))
```

```python
import functools

import numpy as np
import jax
import jax.numpy as jnp
from jax import lax
from jax.experimental import pallas as pl
from jax.experimental.pallas import tpu as pltpu

D_MODEL = 1024
D_FF = 2816
DEPTH = 2
FNET_GROUPS = 8
FNET_GROUP_DIM = D_MODEL // FNET_GROUPS
ALPHA = float((2 * DEPTH) ** 0.25)
LN_EPS = 1e-5

F32 = jnp.float32
BF16 = jnp.bfloat16

V7X_VMEM_BYTES = 64 * 1024 * 1024
VMEM_LIMIT_BYTES = V7X_VMEM_BYTES - 8 * 1024 * 1024
SUBLANES = 8
LANES = 128

DFT_N2 = 128
FFN_TOKENS = SUBLANES * DFT_N2
FFN_SUB_TOKENS = 256
CONV_TOKENS = 1024
CONV_SUB_TOKENS = 256
FNET_TOKENS = 1024
FNET_SUB_TOKENS = 256


def _layer_norm(z, g, b):
    mu = jnp.mean(z, axis=-1, keepdims=True)
    zc = z - mu
    var = jnp.mean(zc * zc, axis=-1, keepdims=True)
    return zc * lax.rsqrt(var + LN_EPS) * g + b


def _dot(a, b):
    return jnp.dot(a, b, preferred_element_type=F32)


def _resident(arr, *lead):
    rest = arr.shape[len(lead):]
    index = tuple(lead) + (0,) * len(rest)
    return pl.BlockSpec((None,) * len(lead) + rest, lambda *_: index, pipeline_mode=pl.Buffered(1))


def _params(n_axes):
    return pltpu.CompilerParams(
        dimension_semantics=("arbitrary",) * n_axes,
        vmem_limit_bytes=VMEM_LIMIT_BYTES,
    )


def _ffn_kernel(x_ref, wg_ref, wu_ref, wd_ref, g_ref, b_ref, o_ref, *, swapped_out):
    n2, sub = DFT_N2, FFN_SUB_TOKENS
    span = sub // SUBLANES
    wg, wu, wd = wg_ref[...], wu_ref[...], wd_ref[...]
    g, b = g_ref[...], b_ref[...]
    for s in range(FFN_TOKENS // sub):
        if swapped_out:
            x = jnp.concatenate([x_ref[r * n2 + s * span:r * n2 + (s + 1) * span, :]
                                 for r in range(SUBLANES)], axis=0)
        else:
            x = x_ref[s * sub:(s + 1) * sub, :]
        xb = x.astype(BF16)
        gate = _dot(xb, wg)
        up = _dot(xb, wu)
        h = (gate * jax.nn.sigmoid(gate) * up).astype(BF16)
        y = _layer_norm(ALPHA * x + 0.5 * _dot(h, wd), g, b)
        if swapped_out:
            o_ref[s * span:(s + 1) * span] = jnp.swapaxes(y.reshape(SUBLANES, span, y.shape[-1]), 0, 1)
        else:
            o_ref[s * sub:(s + 1) * sub, :] = y


def _ffn_ln(x2d, w, layer, half, *, bsz, seq_len, swapped_out=False):
    d = D_MODEL
    n = bsz * seq_len
    tm = FFN_TOKENS
    n1 = seq_len // DFT_N2
    tiles_per_seq = seq_len // tm
    assert seq_len % tm == 0 and n1 == tiles_per_seq * SUBLANES
    nat = pl.BlockSpec((tm, d), lambda i: (i, 0))
    swp = pl.BlockSpec((None, DFT_N2, SUBLANES, d),
                       lambda i: (i // tiles_per_seq, 0, i % tiles_per_seq, 0))
    nat_shape = jax.ShapeDtypeStruct((n, d), F32)
    swp_shape = jax.ShapeDtypeStruct((bsz, DFT_N2, n1, d), F32)
    ln = 2 * half
    return pl.pallas_call(
        functools.partial(_ffn_kernel, swapped_out=swapped_out),
        grid=(n // tm,),
        in_specs=[nat, _resident(w["wg"], layer, half), _resident(w["wu"], layer, half),
                  _resident(w["wd"], layer, half), _resident(w["ln_g"], layer, ln),
                  _resident(w["ln_b"], layer, ln)],
        out_specs=swp if swapped_out else nat,
        out_shape=swp_shape if swapped_out else nat_shape,
        compiler_params=_params(1),
        name="ffn_ln",
    )(x2d, w["wg"], w["wu"], w["wd"], w["ln_g"], w["ln_b"])


def _conv_kernel(x_ref, xp_ref, xn_ref, win_ref, cw_ref, wout_ref, g_ref, b_ref, o_ref,
                 *, tiles_per_seq):
    tm, d = x_ref.shape
    sub = CONV_SUB_TOKENS
    n_sub = tm // sub
    i = pl.program_id(0)
    first = (i % tiles_per_seq) == 0
    last = (i % tiles_per_seq) == tiles_per_seq - 1
    xp = jnp.where(first, 0.0, xp_ref[...])
    xn = jnp.where(last, 0.0, xn_ref[...])
    w_b, w_c, w_h = win_ref[:, :d], win_ref[:, d:2 * d], win_ref[:, 2 * d:]
    wout, cw, g, b = wout_ref[...], cw_ref[...], g_ref[...], b_ref[...]
    rows = sub + 2 * SUBLANES
    for s in range(n_sub):
        lo, hi = s * sub, (s + 1) * sub
        x = x_ref[lo:hi, :]
        before = xp if s == 0 else x_ref[lo - SUBLANES:lo, :]
        after = xn if s == n_sub - 1 else x_ref[hi:hi + SUBLANES, :]
        xa = jnp.concatenate([before, x, after], axis=0).astype(BF16)
        gate_b = _dot(xa[SUBLANES:SUBLANES + sub], w_b)
        v = _dot(xa, w_c) * _dot(xa, w_h)
        v_prev = pltpu.roll(v, 1, axis=0)[SUBLANES:SUBLANES + sub]
        v_next = pltpu.roll(v, rows - 1, axis=0)[SUBLANES:SUBLANES + sub]
        v_mid = v[SUBLANES:SUBLANES + sub]
        conv = v_prev * cw[0:1] + v_mid * cw[1:2] + v_next * cw[2:3]
        y = _dot((gate_b * conv).astype(BF16), wout)
        o_ref[lo:hi, :] = _layer_norm(ALPHA * x + y, g, b)


def _conv_ln(x2d, seq_len, w, layer, j):
    n, d = x2d.shape
    tm = CONV_TOKENS
    assert seq_len % tm == 0 and tm % CONV_SUB_TOKENS == 0
    halo_per_tile = tm // SUBLANES
    n_halo_blocks = n // SUBLANES
    row = pl.BlockSpec((tm, d), lambda i: (i, 0))
    prev = pl.BlockSpec((SUBLANES, d), lambda i: (jnp.maximum(i * halo_per_tile - 1, 0), 0))
    nxt = pl.BlockSpec((SUBLANES, d),
                       lambda i: (jnp.minimum((i + 1) * halo_per_tile, n_halo_blocks - 1), 0))
    return pl.pallas_call(
        functools.partial(_conv_kernel, tiles_per_seq=seq_len // tm),
        grid=(n // tm,),
        in_specs=[row, prev, nxt, _resident(w["conv_in"], j), _resident(w["conv_w"], j),
                  _resident(w["conv_out"], j), _resident(w["ln_g"], layer, 1),
                  _resident(w["ln_b"], layer, 1)],
        out_specs=row,
        out_shape=jax.ShapeDtypeStruct((n, d), F32),
        compiler_params=_params(1),
        name="conv_ln",
    )(x2d, x2d, x2d, w["conv_in"], w["conv_w"], w["conv_out"], w["ln_g"], w["ln_b"])


def _dft_cos_sin(rows, cols, n):
    ang = 2.0 * np.pi * ((rows[:, None] * cols[None, :]) % n) / n
    return np.cos(ang), np.sin(ang)


def _fnet_constants(seq_len):
    n2 = DFT_N2
    n1 = seq_len // n2
    m = n2 // n1
    gd = np.arange(FNET_GROUP_DIM)
    cg, sg = _dft_cos_sin(gd, gd, FNET_GROUP_DIM)
    gcs = np.concatenate([cg, sg], axis=1) / np.sqrt(FNET_GROUP_DIM)
    i1 = np.arange(n1)
    c1, s1 = _dft_cos_sin(i1, i1, n1)
    m1 = np.block([[c1, -s1], [s1, c1]]) / np.sqrt(n1)
    k2 = (m * np.arange(n1)[None, :] + np.arange(m)[:, None]).reshape(-1)
    c2, s2 = _dft_cos_sin(k2, np.arange(n2), n2)
    m2 = np.concatenate([c2, -s2], axis=1) / np.sqrt(n2)
    ang = 2.0 * np.pi * ((np.arange(n2)[:, None] * i1[None, :]) % seq_len) / seq_len
    twc = jnp.broadcast_to(jnp.asarray(np.cos(ang), F32)[:, :, None], (n2, n1, LANES))
    tws = jnp.broadcast_to(jnp.asarray(np.sin(ang), F32)[:, :, None], (n2, n1, LANES))
    return (jnp.asarray(gcs, F32).astype(BF16), jnp.asarray(m1, F32).astype(BF16),
            jnp.asarray(m2, F32).astype(BF16), twc, tws)


def _fnet_stage1_kernel(x_ref, win_ref, gcs_ref, m1_ref, twc_ref, tws_ref, tr_ref, ti_ref):
    cols, n1, d = x_ref.shape
    gd = FNET_GROUP_DIM
    sub_cols = FNET_SUB_TOKENS // n1
    win, gcs, m1 = win_ref[...], gcs_ref[...], m1_ref[...]
    res, ims = [], []
    for c0 in range(0, cols, sub_cols):
        xs = x_ref[c0:c0 + sub_cols].reshape(sub_cols * n1, d).astype(BF16)
        u = _dot(xs, win).astype(BF16)
        pq = [_dot(u[:, g * gd:(g + 1) * gd], gcs) for g in range(FNET_GROUPS)]
        p = jnp.concatenate([t[:, :gd] for t in pq], axis=1).astype(BF16)
        q = jnp.concatenate([t[:, gd:] for t in pq], axis=1).astype(BF16)
        for j in range(sub_cols):
            rhs = jnp.concatenate([p[j * n1:(j + 1) * n1], q[j * n1:(j + 1) * n1]], axis=0)
            t = _dot(m1, rhs)
            t_re, t_im = t[:n1], t[n1:]
            c = jnp.concatenate([twc_ref[c0 + j]] * (d // LANES), axis=1)
            s = jnp.concatenate([tws_ref[c0 + j]] * (d // LANES), axis=1)
            res.append(t_re * c - t_im * s)
            ims.append(t_re * s + t_im * c)
            if len(res) == SUBLANES:
                lo = c0 + j + 1 - SUBLANES
                tr_ref[:, lo:lo + SUBLANES, :] = jnp.swapaxes(jnp.stack(res, axis=0), 0, 1)
                ti_ref[:, lo:lo + SUBLANES, :] = jnp.swapaxes(jnp.stack(ims, axis=0), 0, 1)
                res, ims = [], []


def _fnet_stage2_kernel(tr_ref, ti_ref, x_ref, m2_ref, wout_ref, g_ref, b_ref, o_ref):
    m, cols, n1, d = x_ref.shape
    m2 = m2_ref[...]
    fs = []
    for k in range(cols):
        rhs = jnp.concatenate([tr_ref[k].astype(BF16), ti_ref[k].astype(BF16)], axis=0)
        fs.append(_dot(m2, rhs))
    f = jnp.concatenate(fs, axis=0).astype(BF16)
    mix = _dot(f, wout_ref[...])
    g = g_ref[...]
    b = b_ref[...]
    n2 = m * n1
    for h in range(m):
        ys = [_layer_norm(ALPHA * x_ref[h, k] + mix[k * n2 + h * n1:k * n2 + (h + 1) * n1], g, b)
              for k in range(cols)]
        o_ref[:, h, :, :] = jnp.swapaxes(jnp.stack(ys, axis=0), 0, 1)


def _fnet_ln(xs4, w, layer, j):
    bsz, n2, n1, d = xs4.shape
    assert n2 == DFT_N2 and n2 % n1 == 0 and d == D_MODEL
    m = n2 // n1
    gcs, m1, m2, twc, tws = _fnet_constants(n1 * n2)

    ca = FNET_TOKENS // n1
    assert n2 % ca == 0 and FNET_SUB_TOKENS % n1 == 0
    x_blk = pl.BlockSpec((None, ca, n1, d), lambda bi, c: (bi, c, 0, 0))
    tw_blk = pl.BlockSpec((ca, n1, LANES), lambda bi, c: (c, 0, 0))
    t_out = pl.BlockSpec((None, n1, ca, d), lambda bi, c: (bi, 0, c, 0))
    t_shape = jax.ShapeDtypeStruct((bsz, n1, n2, d), F32)
    t_re, t_im = pl.pallas_call(
        _fnet_stage1_kernel,
        grid=(bsz, n2 // ca),
        in_specs=[x_blk, _resident(w["fnet_in"], j), _resident(gcs), _resident(m1), tw_blk, tw_blk],
        out_specs=[t_out, t_out],
        out_shape=[t_shape, t_shape],
        compiler_params=_params(2),
        name="fnet_stage1",
    )(xs4, w["fnet_in"], gcs, m1, twc, tws)

    cb = FNET_TOKENS // n2
    assert n1 % cb == 0 and cb % SUBLANES == 0
    xs5 = xs4.reshape(bsz, m, n1, n1, d)
    t_in = pl.BlockSpec((None, cb, n2, d), lambda bi, k: (bi, k, 0, 0))
    r_blk = pl.BlockSpec((None, m, cb, n1, d), lambda bi, k: (bi, 0, k, 0, 0))
    o_blk = pl.BlockSpec((None, n1, m, cb, d), lambda bi, k: (bi, 0, 0, k, 0))
    out = pl.pallas_call(
        _fnet_stage2_kernel,
        grid=(bsz, n1 // cb),
        in_specs=[t_in, t_in, r_blk, _resident(m2), _resident(w["fnet_out"], j),
                  _resident(w["ln_g"], layer, 1), _resident(w["ln_b"], layer, 1)],
        out_specs=o_blk,
        out_shape=jax.ShapeDtypeStruct((bsz, n1, m, n1, d), F32),
        compiler_params=_params(2),
        name="fnet_stage2",
    )(t_re, t_im, xs5, m2, w["fnet_out"], w["ln_g"], w["ln_b"])
    return out.reshape(bsz, n1 * n2, d)


def _trunk(x, w):
    bsz, s, d = x.shape
    assert DEPTH == 2 and d == D_MODEL
    ffn = functools.partial(_ffn_ln, bsz=bsz, seq_len=s)
    x = x.reshape(bsz * s, d)
    xs = ffn(x, w, 0, 0, swapped_out=True)
    x = _fnet_ln(xs, w, 0, 0).reshape(bsz * s, d)
    x = ffn(x, w, 0, 1)
    x = ffn(x, w, 1, 0)
    x = _conv_ln(x, s, w, 1, 0)
    x = ffn(x, w, 1, 1)
    return x.reshape(bsz, s, d)


def kernel(x_prompt, x_sample, ffn_w_gate, ffn_w_up, ffn_w_down, ln_g, ln_b,
           fnet_w_in, fnet_w_out, conv_w_in, conv_w, conv_w_out):
    n_layers, n_ln, d = ln_g.shape
    w = {
        "wg": ffn_w_gate.astype(BF16),
        "wu": ffn_w_up.astype(BF16),
        "wd": ffn_w_down.astype(BF16),
        "ln_g": ln_g.reshape(n_layers, n_ln, 1, d),
        "ln_b": ln_b.reshape(n_layers, n_ln, 1, d),
        "fnet_in": fnet_w_in.astype(BF16),
        "fnet_out": fnet_w_out.astype(BF16),
        "conv_in": conv_w_in.astype(BF16),
        "conv_w": conv_w,
        "conv_out": conv_w_out.astype(BF16),
    }
    return (_trunk(x_prompt, w), _trunk(x_sample, w))
```

```python
import functools

import numpy as np
import jax
import jax.numpy as jnp
from jax import lax
from jax.experimental import pallas as pl
from jax.experimental.pallas import tpu as pltpu

D_MODEL = 1024
D_FF = 2816
DEPTH = 2
FNET_GROUPS = 8
FNET_GROUP_DIM = D_MODEL // FNET_GROUPS
ALPHA = float((2 * DEPTH) ** 0.25)
LN_EPS = 1e-5

F32 = jnp.float32
BF16 = jnp.bfloat16

V7X_VMEM_BYTES = 64 * 1024 * 1024
VMEM_LIMIT_BYTES = V7X_VMEM_BYTES - 8 * 1024 * 1024
SUBLANES = 8
BF16_ROWS = 16
LANES = 128

DFT_N2 = 128
FFN_TOKENS = SUBLANES * DFT_N2
FFN_SUB_TOKENS = 256
CONV_TOKENS = 1024
CONV_SUB_TOKENS = 256
FNET_TOKENS = 1024
FNET_SUB_TOKENS = 256


def _layer_norm(z, g, b):
    mu = jnp.mean(z, axis=-1, keepdims=True)
    zc = z - mu
    var = jnp.mean(zc * zc, axis=-1, keepdims=True)
    return zc * lax.rsqrt(var + LN_EPS) * g + b


def _dot(a, b):
    return jnp.dot(a, b, preferred_element_type=F32)


def _resident(arr, *lead):
    rest = arr.shape[len(lead):]
    index = tuple(lead) + (0,) * len(rest)
    return pl.BlockSpec((None,) * len(lead) + rest, lambda *_: index, pipeline_mode=pl.Buffered(1))


def _params(n_axes):
    return pltpu.CompilerParams(
        dimension_semantics=("arbitrary",) * n_axes,
        vmem_limit_bytes=VMEM_LIMIT_BYTES,
    )


def _ffn_kernel(x_ref, wg_ref, wu_ref, wd_ref, g_ref, b_ref, o_ref, *, swapped_out):
    n2, sub = DFT_N2, FFN_SUB_TOKENS
    span = sub // SUBLANES
    wg, wu, wd = wg_ref[...], wu_ref[...], wd_ref[...]
    g, b = g_ref[...], b_ref[...]
    for s in range(FFN_TOKENS // sub):
        if swapped_out:
            x = jnp.concatenate([x_ref[r * n2 + s * span:r * n2 + (s + 1) * span, :]
                                 for r in range(SUBLANES)], axis=0)
        else:
            x = x_ref[s * sub:(s + 1) * sub, :]
        xb = x.astype(BF16)
        gate = _dot(xb, wg)
        up = _dot(xb, wu)
        h = (gate * jax.nn.sigmoid(gate) * up).astype(BF16)
        y = _layer_norm(ALPHA * x + 0.5 * _dot(h, wd), g, b)
        if swapped_out:
            o_ref[s * span:(s + 1) * span] = jnp.swapaxes(y.reshape(SUBLANES, span, y.shape[-1]), 0, 1)
        else:
            o_ref[s * sub:(s + 1) * sub, :] = y


def _ffn_ln(x2d, w, layer, half, *, bsz, seq_len, swapped_out=False):
    d = D_MODEL
    n = bsz * seq_len
    tm = FFN_TOKENS
    n1 = seq_len // DFT_N2
    tiles_per_seq = seq_len // tm
    assert seq_len % tm == 0 and n1 == tiles_per_seq * SUBLANES
    nat = pl.BlockSpec((tm, d), lambda i: (i, 0))
    swp = pl.BlockSpec((None, DFT_N2, SUBLANES, d),
                       lambda i: (i // tiles_per_seq, 0, i % tiles_per_seq, 0))
    nat_shape = jax.ShapeDtypeStruct((n, d), F32)
    swp_shape = jax.ShapeDtypeStruct((bsz, DFT_N2, n1, d), F32)
    ln = 2 * half
    return pl.pallas_call(
        functools.partial(_ffn_kernel, swapped_out=swapped_out),
        grid=(n // tm,),
        in_specs=[nat, _resident(w["wg"], layer, half), _resident(w["wu"], layer, half),
                  _resident(w["wd"], layer, half), _resident(w["ln_g"], layer, ln),
                  _resident(w["ln_b"], layer, ln)],
        out_specs=swp if swapped_out else nat,
        out_shape=swp_shape if swapped_out else nat_shape,
        compiler_params=_params(1),
        name="ffn_ln",
    )(x2d, w["wg"], w["wu"], w["wd"], w["ln_g"], w["ln_b"])


def _conv_kernel(x_ref, xp_ref, xn_ref, win_ref, cw_ref, wout_ref, g_ref, b_ref, o_ref,
                 *, tiles_per_seq):
    tm, d = x_ref.shape
    sub = CONV_SUB_TOKENS
    n_sub = tm // sub
    i = pl.program_id(0)
    first = (i % tiles_per_seq) == 0
    last = (i % tiles_per_seq) == tiles_per_seq - 1
    xp = jnp.where(first, 0.0, xp_ref[...])
    xn = jnp.where(last, 0.0, xn_ref[...])
    w_b, w_c, w_h = win_ref[:, :d], win_ref[:, d:2 * d], win_ref[:, 2 * d:]
    wout, cw, g, b = wout_ref[...], cw_ref[...], g_ref[...], b_ref[...]
    rows = sub + 2 * SUBLANES
    for s in range(n_sub):
        lo, hi = s * sub, (s + 1) * sub
        x = x_ref[lo:hi, :]
        before = xp if s == 0 else x_ref[lo - SUBLANES:lo, :]
        after = xn if s == n_sub - 1 else x_ref[hi:hi + SUBLANES, :]
        xa = jnp.concatenate([before, x, after], axis=0).astype(BF16)
        gate_b = _dot(xa[SUBLANES:SUBLANES + sub], w_b)
        v = _dot(xa, w_c) * _dot(xa, w_h)
        v_prev = pltpu.roll(v, 1, axis=0)[SUBLANES:SUBLANES + sub]
        v_next = pltpu.roll(v, rows - 1, axis=0)[SUBLANES:SUBLANES + sub]
        v_mid = v[SUBLANES:SUBLANES + sub]
        conv = v_prev * cw[0:1] + v_mid * cw[1:2] + v_next * cw[2:3]
        y = _dot((gate_b * conv).astype(BF16), wout)
        o_ref[lo:hi, :] = _layer_norm(ALPHA * x + y, g, b)


def _conv_ln(x2d, seq_len, w, layer, j):
    n, d = x2d.shape
    tm = CONV_TOKENS
    assert seq_len % tm == 0 and tm % CONV_SUB_TOKENS == 0
    halo_per_tile = tm // SUBLANES
    n_halo_blocks = n // SUBLANES
    row = pl.BlockSpec((tm, d), lambda i: (i, 0))
    prev = pl.BlockSpec((SUBLANES, d), lambda i: (jnp.maximum(i * halo_per_tile - 1, 0), 0))
    nxt = pl.BlockSpec((SUBLANES, d),
                       lambda i: (jnp.minimum((i + 1) * halo_per_tile, n_halo_blocks - 1), 0))
    return pl.pallas_call(
        functools.partial(_conv_kernel, tiles_per_seq=seq_len // tm),
        grid=(n // tm,),
        in_specs=[row, prev, nxt, _resident(w["conv_in"], j), _resident(w["conv_w"], j),
                  _resident(w["conv_out"], j), _resident(w["ln_g"], layer, 1),
                  _resident(w["ln_b"], layer, 1)],
        out_specs=row,
        out_shape=jax.ShapeDtypeStruct((n, d), F32),
        compiler_params=_params(1),
        name="conv_ln",
    )(x2d, x2d, x2d, w["conv_in"], w["conv_w"], w["conv_out"], w["ln_g"], w["ln_b"])


def _dft_cos_sin(rows, cols, n):
    ang = 2.0 * np.pi * ((rows[:, None] * cols[None, :]) % n) / n
    return np.cos(ang), np.sin(ang)


def _fnet_constants(seq_len):
    n2 = DFT_N2
    n1 = seq_len // n2
    m = n2 // n1
    gd = np.arange(FNET_GROUP_DIM)
    cg, sg = _dft_cos_sin(gd, gd, FNET_GROUP_DIM)
    gcs = np.concatenate([cg, sg], axis=1) / np.sqrt(FNET_GROUP_DIM)
    i1 = np.arange(n1)
    c1, s1 = _dft_cos_sin(i1, i1, n1)
    m1 = np.block([[c1, -s1], [s1, c1]]) / np.sqrt(n1)
    k2 = (m * np.arange(n1)[None, :] + np.arange(m)[:, None]).reshape(-1)
    c2, s2 = _dft_cos_sin(k2, np.arange(n2), n2)
    m2 = np.concatenate([c2, -s2], axis=1) / np.sqrt(n2)
    ang = 2.0 * np.pi * ((np.arange(n2)[:, None] * i1[None, :]) % seq_len) / seq_len
    twc = jnp.broadcast_to(jnp.asarray(np.cos(ang), F32)[:, :, None], (n2, n1, LANES))
    tws = jnp.broadcast_to(jnp.asarray(np.sin(ang), F32)[:, :, None], (n2, n1, LANES))
    return (jnp.asarray(gcs, F32).astype(BF16), jnp.asarray(m1, F32).astype(BF16),
            jnp.asarray(m2, F32).astype(BF16), twc, tws)


def _fnet_stage1_kernel(x_ref, win_ref, gcs_ref, m1_ref, twc_ref, tws_ref, tr_ref, ti_ref):
    cols, n1, d = x_ref.shape
    gd = FNET_GROUP_DIM
    sub_cols = FNET_SUB_TOKENS // n1
    win, gcs, m1 = win_ref[...], gcs_ref[...], m1_ref[...]
    res, ims = [], []
    for c0 in range(0, cols, sub_cols):
        xs = x_ref[c0:c0 + sub_cols].reshape(sub_cols * n1, d).astype(BF16)
        u = _dot(xs, win).astype(BF16)
        pq = [_dot(u[:, g * gd:(g + 1) * gd], gcs) for g in range(FNET_GROUPS)]
        p = jnp.concatenate([t[:, :gd] for t in pq], axis=1).astype(BF16)
        q = jnp.concatenate([t[:, gd:] for t in pq], axis=1).astype(BF16)
        for j in range(sub_cols):
            rhs = jnp.concatenate([p[j * n1:(j + 1) * n1], q[j * n1:(j + 1) * n1]], axis=0)
            t = _dot(m1, rhs)
            t_re, t_im = t[:n1], t[n1:]
            c = jnp.concatenate([twc_ref[c0 + j]] * (d // LANES), axis=1)
            s = jnp.concatenate([tws_ref[c0 + j]] * (d // LANES), axis=1)
            res.append((t_re * c - t_im * s).astype(BF16))
            ims.append((t_re * s + t_im * c).astype(BF16))
            if len(res) == BF16_ROWS:
                lo = c0 + j + 1 - BF16_ROWS
                tr_ref[:, lo:lo + BF16_ROWS, :] = jnp.swapaxes(jnp.stack(res, axis=0), 0, 1)
                ti_ref[:, lo:lo + BF16_ROWS, :] = jnp.swapaxes(jnp.stack(ims, axis=0), 0, 1)
                res, ims = [], []


def _fnet_stage2_kernel(tr_ref, ti_ref, x_ref, m2_ref, wout_ref, g_ref, b_ref, o_ref):
    m, cols, n1, d = x_ref.shape
    m2 = m2_ref[...]
    fs = []
    for k in range(cols):
        rhs = jnp.concatenate([tr_ref[k], ti_ref[k]], axis=0)
        fs.append(_dot(m2, rhs))
    f = jnp.concatenate(fs, axis=0).astype(BF16)
    mix = _dot(f, wout_ref[...])
    g = g_ref[...]
    b = b_ref[...]
    n2 = m * n1
    for h in range(m):
        ys = [_layer_norm(ALPHA * x_ref[h, k] + mix[k * n2 + h * n1:k * n2 + (h + 1) * n1], g, b)
              for k in range(cols)]
        o_ref[:, h, :, :] = jnp.swapaxes(jnp.stack(ys, axis=0), 0, 1)


def _fnet_ln(xs4, w, layer, j):
    bsz, n2, n1, d = xs4.shape
    assert n2 == DFT_N2 and n2 % n1 == 0 and d == D_MODEL
    m = n2 // n1
    gcs, m1, m2, twc, tws = _fnet_constants(n1 * n2)

    ca = max(FNET_TOKENS // n1, BF16_ROWS)
    assert n2 % ca == 0 and FNET_SUB_TOKENS % n1 == 0
    x_blk = pl.BlockSpec((None, ca, n1, d), lambda bi, c: (bi, c, 0, 0))
    tw_blk = pl.BlockSpec((ca, n1, LANES), lambda bi, c: (c, 0, 0))
    t_out = pl.BlockSpec((None, n1, ca, d), lambda bi, c: (bi, 0, c, 0))
    t_shape = jax.ShapeDtypeStruct((bsz, n1, n2, d), BF16)
    t_re, t_im = pl.pallas_call(
        _fnet_stage1_kernel,
        grid=(bsz, n2 // ca),
        in_specs=[x_blk, _resident(w["fnet_in"], j), _resident(gcs), _resident(m1), tw_blk, tw_blk],
        out_specs=[t_out, t_out],
        out_shape=[t_shape, t_shape],
        compiler_params=_params(2),
        name="fnet_stage1",
    )(xs4, w["fnet_in"], gcs, m1, twc, tws)

    cb = FNET_TOKENS // n2
    assert n1 % cb == 0 and cb % SUBLANES == 0
    xs5 = xs4.reshape(bsz, m, n1, n1, d)
    t_in = pl.BlockSpec((None, cb, n2, d), lambda bi, k: (bi, k, 0, 0))
    r_blk = pl.BlockSpec((None, m, cb, n1, d), lambda bi, k: (bi, 0, k, 0, 0))
    o_blk = pl.BlockSpec((None, n1, m, cb, d), lambda bi, k: (bi, 0, 0, k, 0))
    out = pl.pallas_call(
        _fnet_stage2_kernel,
        grid=(bsz, n1 // cb),
        in_specs=[t_in, t_in, r_blk, _resident(m2), _resident(w["fnet_out"], j),
                  _resident(w["ln_g"], layer, 1), _resident(w["ln_b"], layer, 1)],
        out_specs=o_blk,
        out_shape=jax.ShapeDtypeStruct((bsz, n1, m, n1, d), F32),
        compiler_params=_params(2),
        name="fnet_stage2",
    )(t_re, t_im, xs5, m2, w["fnet_out"], w["ln_g"], w["ln_b"])
    return out.reshape(bsz, n1 * n2, d)


def _trunk(x, w):
    bsz, s, d = x.shape
    assert DEPTH == 2 and d == D_MODEL
    ffn = functools.partial(_ffn_ln, bsz=bsz, seq_len=s)
    x = x.reshape(bsz * s, d)
    xs = ffn(x, w, 0, 0, swapped_out=True)
    x = _fnet_ln(xs, w, 0, 0).reshape(bsz * s, d)
    x = ffn(x, w, 0, 1)
    x = ffn(x, w, 1, 0)
    x = _conv_ln(x, s, w, 1, 0)
    x = ffn(x, w, 1, 1)
    return x.reshape(bsz, s, d)


def kernel(x_prompt, x_sample, ffn_w_gate, ffn_w_up, ffn_w_down, ln_g, ln_b,
           fnet_w_in, fnet_w_out, conv_w_in, conv_w, conv_w_out):
    n_layers, n_ln, d = ln_g.shape
    w = {
        "wg": ffn_w_gate.astype(BF16),
        "wu": ffn_w_up.astype(BF16),
        "wd": ffn_w_down.astype(BF16),
        "ln_g": ln_g.reshape(n_layers, n_ln, 1, d),
        "ln_b": ln_b.reshape(n_layers, n_ln, 1, d),
        "fnet_in": fnet_w_in.astype(BF16),
        "fnet_out": fnet_w_out.astype(BF16),
        "conv_in": conv_w_in.astype(BF16),
        "conv_w": conv_w,
        "conv_out": conv_w_out.astype(BF16),
    }
    return (_trunk(x_prompt, w), _trunk(x_sample, w))
```

```python
import functools

import numpy as np
import jax
import jax.numpy as jnp
from jax import lax
from jax.experimental import pallas as pl
from jax.experimental.pallas import tpu as pltpu

D_MODEL = 1024
D_FF = 2816
DEPTH = 2
FNET_GROUPS = 8
FNET_GROUP_DIM = D_MODEL // FNET_GROUPS
ALPHA = float((2 * DEPTH) ** 0.25)
LN_EPS = 1e-5

F32 = jnp.float32
BF16 = jnp.bfloat16

V7X_VMEM_BYTES = 64 * 1024 * 1024
VMEM_LIMIT_BYTES = V7X_VMEM_BYTES - 8 * 1024 * 1024
SUBLANES = 8
BF16_ROWS = 16
LANES = 128

DFT_N2 = 128
FFN_TOKENS = SUBLANES * DFT_N2
FFN_SUB_TOKENS = 256
CONV_TOKENS = 1024
CONV_SUB_TOKENS = 256
FNET_TOKENS = 1024
FNET_SUB_TOKENS = 256


def _layer_norm(z, g, b):
    mu = jnp.mean(z, axis=-1, keepdims=True)
    zc = z - mu
    var = jnp.mean(zc * zc, axis=-1, keepdims=True)
    return zc * lax.rsqrt(var + LN_EPS) * g + b


def _dot(a, b):
    return jnp.dot(a, b, preferred_element_type=F32)


def _resident(arr, *lead):
    rest = arr.shape[len(lead):]
    index = tuple(lead) + (0,) * len(rest)
    return pl.BlockSpec((None,) * len(lead) + rest, lambda *_: index, pipeline_mode=pl.Buffered(1))


def _params(n_axes):
    return pltpu.CompilerParams(
        dimension_semantics=("arbitrary",) * n_axes,
        vmem_limit_bytes=VMEM_LIMIT_BYTES,
    )


def _ffn_kernel(x_ref, wg_ref, wu_ref, wd_ref, g_ref, b_ref, o_ref, *, swapped_out):
    n2, sub = DFT_N2, FFN_SUB_TOKENS
    span = sub // SUBLANES
    wg, wu, wd = wg_ref[...], wu_ref[...], wd_ref[...]
    g, b = g_ref[...], b_ref[...]
    for s in range(FFN_TOKENS // sub):
        if swapped_out:
            x = jnp.concatenate([x_ref[r * n2 + s * span:r * n2 + (s + 1) * span, :]
                                 for r in range(SUBLANES)], axis=0)
        else:
            x = x_ref[s * sub:(s + 1) * sub, :]
        xb = x.astype(BF16)
        gate = _dot(xb, wg)
        up = _dot(xb, wu)
        h = (gate * jax.nn.sigmoid(gate) * up).astype(BF16)
        y = _layer_norm(ALPHA * x + 0.5 * _dot(h, wd), g, b)
        if swapped_out:
            o_ref[s * span:(s + 1) * span] = jnp.swapaxes(y.reshape(SUBLANES, span, y.shape[-1]), 0, 1)
        else:
            o_ref[s * sub:(s + 1) * sub, :] = y


def _ffn_ln(x2d, w, layer, half, *, bsz, seq_len, swapped_out=False):
    d = D_MODEL
    n = bsz * seq_len
    tm = FFN_TOKENS
    n1 = seq_len // DFT_N2
    tiles_per_seq = seq_len // tm
    assert seq_len % tm == 0 and n1 == tiles_per_seq * SUBLANES
    nat = pl.BlockSpec((tm, d), lambda i: (i, 0))
    swp = pl.BlockSpec((None, DFT_N2, SUBLANES, d),
                       lambda i: (i // tiles_per_seq, 0, i % tiles_per_seq, 0))
    nat_shape = jax.ShapeDtypeStruct((n, d), F32)
    swp_shape = jax.ShapeDtypeStruct((bsz, DFT_N2, n1, d), F32)
    ln = 2 * half
    return pl.pallas_call(
        functools.partial(_ffn_kernel, swapped_out=swapped_out),
        grid=(n // tm,),
        in_specs=[nat, _resident(w["wg"], layer, half), _resident(w["wu"], layer, half),
                  _resident(w["wd"], layer, half), _resident(w["ln_g"], layer, ln),
                  _resident(w["ln_b"], layer, ln)],
        out_specs=swp if swapped_out else nat,
        out_shape=swp_shape if swapped_out else nat_shape,
        compiler_params=_params(1),
        name="ffn_ln",
    )(x2d, w["wg"], w["wu"], w["wd"], w["ln_g"], w["ln_b"])


def _conv_kernel(x_ref, xp_ref, xn_ref, win_ref, cw_ref, wout_ref, g_ref, b_ref, o_ref,
                 *, tiles_per_seq):
    tm, d = x_ref.shape
    sub = CONV_SUB_TOKENS
    n_sub = tm // sub
    i = pl.program_id(0)
    first = (i % tiles_per_seq) == 0
    last = (i % tiles_per_seq) == tiles_per_seq - 1
    xp = jnp.where(first, 0.0, xp_ref[...])
    xn = jnp.where(last, 0.0, xn_ref[...])
    w_b, w_c, w_h = win_ref[:, :d], win_ref[:, d:2 * d], win_ref[:, 2 * d:]
    wout, cw, g, b = wout_ref[...], cw_ref[...], g_ref[...], b_ref[...]
    rows = sub + 2 * SUBLANES
    vs = []
    for s in range(n_sub):
        parts = [x_ref[s * sub:(s + 1) * sub, :]]
        if s == 0:
            parts.insert(0, xp)
        if s == n_sub - 1:
            parts.append(xn)
        xc = jnp.concatenate(parts, axis=0).astype(BF16)
        vs.append(_dot(xc, w_c) * _dot(xc, w_h))
    for s in range(n_sub):
        lo, hi = s * sub, (s + 1) * sub
        x = x_ref[lo:hi, :]
        pieces = [vs[s]]
        if s > 0:
            pieces.insert(0, vs[s - 1][-SUBLANES:])
        if s < n_sub - 1:
            pieces.append(vs[s + 1][:SUBLANES])
        v = jnp.concatenate(pieces, axis=0)
        gate_b = _dot(x.astype(BF16), w_b)
        v_prev = pltpu.roll(v, 1, axis=0)[SUBLANES:SUBLANES + sub]
        v_next = pltpu.roll(v, rows - 1, axis=0)[SUBLANES:SUBLANES + sub]
        v_mid = v[SUBLANES:SUBLANES + sub]
        conv = v_prev * cw[0:1] + v_mid * cw[1:2] + v_next * cw[2:3]
        y = _dot((gate_b * conv).astype(BF16), wout)
        o_ref[lo:hi, :] = _layer_norm(ALPHA * x + y, g, b)


def _conv_ln(x2d, seq_len, w, layer, j):
    n, d = x2d.shape
    tm = CONV_TOKENS
    assert seq_len % tm == 0 and tm % CONV_SUB_TOKENS == 0
    halo_per_tile = tm // SUBLANES
    n_halo_blocks = n // SUBLANES
    row = pl.BlockSpec((tm, d), lambda i: (i, 0))
    prev = pl.BlockSpec((SUBLANES, d), lambda i: (jnp.maximum(i * halo_per_tile - 1, 0), 0))
    nxt = pl.BlockSpec((SUBLANES, d),
                       lambda i: (jnp.minimum((i + 1) * halo_per_tile, n_halo_blocks - 1), 0))
    return pl.pallas_call(
        functools.partial(_conv_kernel, tiles_per_seq=seq_len // tm),
        grid=(n // tm,),
        in_specs=[row, prev, nxt, _resident(w["conv_in"], j), _resident(w["conv_w"], j),
                  _resident(w["conv_out"], j), _resident(w["ln_g"], layer, 1),
                  _resident(w["ln_b"], layer, 1)],
        out_specs=row,
        out_shape=jax.ShapeDtypeStruct((n, d), F32),
        compiler_params=_params(1),
        name="conv_ln",
    )(x2d, x2d, x2d, w["conv_in"], w["conv_w"], w["conv_out"], w["ln_g"], w["ln_b"])


def _dft_cos_sin(rows, cols, n):
    ang = 2.0 * np.pi * ((rows[:, None] * cols[None, :]) % n) / n
    return np.cos(ang), np.sin(ang)


def _fnet_constants(seq_len):
    n2 = DFT_N2
    n1 = seq_len // n2
    m = n2 // n1
    gd = np.arange(FNET_GROUP_DIM)
    cg, sg = _dft_cos_sin(gd, gd, FNET_GROUP_DIM)
    gcs = np.concatenate([cg, sg], axis=1) / np.sqrt(FNET_GROUP_DIM)
    i1 = np.arange(n1)
    c1, s1 = _dft_cos_sin(i1, i1, n1)
    m1 = np.block([[c1, -s1], [s1, c1]]) / np.sqrt(n1)
    k2 = (m * np.arange(n1)[None, :] + np.arange(m)[:, None]).reshape(-1)
    c2, s2 = _dft_cos_sin(k2, np.arange(n2), n2)
    m2 = np.concatenate([c2, -s2], axis=1) / np.sqrt(n2)
    ang = 2.0 * np.pi * ((np.arange(n2)[:, None] * i1[None, :]) % seq_len) / seq_len
    twc = jnp.broadcast_to(jnp.asarray(np.cos(ang), F32)[:, :, None], (n2, n1, LANES))
    tws = jnp.broadcast_to(jnp.asarray(np.sin(ang), F32)[:, :, None], (n2, n1, LANES))
    return (jnp.asarray(gcs, F32).astype(BF16), jnp.asarray(m1, F32).astype(BF16),
            jnp.asarray(m2, F32).astype(BF16), twc, tws)


def _fnet_stage1_kernel(x_ref, win_ref, gcs_ref, m1_ref, twc_ref, tws_ref, tr_ref, ti_ref):
    cols, n1, d = x_ref.shape
    gd = FNET_GROUP_DIM
    sub_cols = FNET_SUB_TOKENS // n1
    win, gcs, m1 = win_ref[...], gcs_ref[...], m1_ref[...]
    res, ims = [], []
    for c0 in range(0, cols, sub_cols):
        xs = x_ref[c0:c0 + sub_cols].reshape(sub_cols * n1, d).astype(BF16)
        u = _dot(xs, win).astype(BF16)
        pq = [_dot(u[:, g * gd:(g + 1) * gd], gcs) for g in range(FNET_GROUPS)]
        p = jnp.concatenate([t[:, :gd] for t in pq], axis=1).astype(BF16)
        q = jnp.concatenate([t[:, gd:] for t in pq], axis=1).astype(BF16)
        for j in range(sub_cols):
            rhs = jnp.concatenate([p[j * n1:(j + 1) * n1], q[j * n1:(j + 1) * n1]], axis=0)
            t = _dot(m1, rhs)
            t_re, t_im = t[:n1], t[n1:]
            c = jnp.concatenate([twc_ref[c0 + j]] * (d // LANES), axis=1)
            s = jnp.concatenate([tws_ref[c0 + j]] * (d // LANES), axis=1)
            res.append((t_re * c - t_im * s).astype(BF16))
            ims.append((t_re * s + t_im * c).astype(BF16))
            if len(res) == BF16_ROWS:
                lo = c0 + j + 1 - BF16_ROWS
                tr_ref[:, lo:lo + BF16_ROWS, :] = jnp.swapaxes(jnp.stack(res, axis=0), 0, 1)
                ti_ref[:, lo:lo + BF16_ROWS, :] = jnp.swapaxes(jnp.stack(ims, axis=0), 0, 1)
                res, ims = [], []


def _fnet_stage2_kernel(tr_ref, ti_ref, x_ref, m2_ref, wout_ref, g_ref, b_ref, o_ref):
    m, cols, n1, d = x_ref.shape
    m2, wout, g, b = m2_ref[...], wout_ref[...], g_ref[...], b_ref[...]
    n2 = m * n1
    per = FNET_SUB_TOKENS // n2
    ys = [[None] * cols for _ in range(m)]
    for k0 in range(0, cols, per):
        fs = [_dot(m2, jnp.concatenate([tr_ref[k], ti_ref[k]], axis=0)) for k in range(k0, k0 + per)]
        mix = _dot(jnp.concatenate(fs, axis=0).astype(BF16), wout)
        for kk in range(per):
            for h in range(m):
                z = ALPHA * x_ref[h, k0 + kk] + mix[kk * n2 + h * n1:kk * n2 + (h + 1) * n1]
                ys[h][k0 + kk] = _layer_norm(z, g, b)
    for h in range(m):
        o_ref[:, h, :, :] = jnp.swapaxes(jnp.stack(ys[h], axis=0), 0, 1)


def _fnet_ln(xs4, w, layer, j):
    bsz, n2, n1, d = xs4.shape
    assert n2 == DFT_N2 and n2 % n1 == 0 and d == D_MODEL
    m = n2 // n1
    gcs, m1, m2, twc, tws = _fnet_constants(n1 * n2)

    ca = max(FNET_TOKENS // n1, BF16_ROWS)
    assert n2 % ca == 0 and FNET_SUB_TOKENS % n1 == 0
    x_blk = pl.BlockSpec((None, ca, n1, d), lambda bi, c: (bi, c, 0, 0))
    tw_blk = pl.BlockSpec((ca, n1, LANES), lambda bi, c: (c, 0, 0))
    t_out = pl.BlockSpec((None, n1, ca, d), lambda bi, c: (bi, 0, c, 0))
    t_shape = jax.ShapeDtypeStruct((bsz, n1, n2, d), BF16)
    t_re, t_im = pl.pallas_call(
        _fnet_stage1_kernel,
        grid=(bsz, n2 // ca),
        in_specs=[x_blk, _resident(w["fnet_in"], j), _resident(gcs), _resident(m1), tw_blk, tw_blk],
        out_specs=[t_out, t_out],
        out_shape=[t_shape, t_shape],
        compiler_params=_params(2),
        name="fnet_stage1",
    )(xs4, w["fnet_in"], gcs, m1, twc, tws)

    cb = FNET_TOKENS // n2
    assert n1 % cb == 0 and cb % SUBLANES == 0
    xs5 = xs4.reshape(bsz, m, n1, n1, d)
    t_in = pl.BlockSpec((None, cb, n2, d), lambda bi, k: (bi, k, 0, 0))
    r_blk = pl.BlockSpec((None, m, cb, n1, d), lambda bi, k: (bi, 0, k, 0, 0))
    o_blk = pl.BlockSpec((None, n1, m, cb, d), lambda bi, k: (bi, 0, 0, k, 0))
    out = pl.pallas_call(
        _fnet_stage2_kernel,
        grid=(bsz, n1 // cb),
        in_specs=[t_in, t_in, r_blk, _resident(m2), _resident(w["fnet_out"], j),
                  _resident(w["ln_g"], layer, 1), _resident(w["ln_b"], layer, 1)],
        out_specs=o_blk,
        out_shape=jax.ShapeDtypeStruct((bsz, n1, m, n1, d), F32),
        compiler_params=_params(2),
        name="fnet_stage2",
    )(t_re, t_im, xs5, m2, w["fnet_out"], w["ln_g"], w["ln_b"])
    return out.reshape(bsz, n1 * n2, d)


def _trunk(x, w):
    bsz, s, d = x.shape
    assert DEPTH == 2 and d == D_MODEL
    ffn = functools.partial(_ffn_ln, bsz=bsz, seq_len=s)
    x = x.reshape(bsz * s, d)
    xs = ffn(x, w, 0, 0, swapped_out=True)
    x = _fnet_ln(xs, w, 0, 0).reshape(bsz * s, d)
    x = ffn(x, w, 0, 1)
    x = ffn(x, w, 1, 0)
    x = _conv_ln(x, s, w, 1, 0)
    x = ffn(x, w, 1, 1)
    return x.reshape(bsz, s, d)


def kernel(x_prompt, x_sample, ffn_w_gate, ffn_w_up, ffn_w_down, ln_g, ln_b,
           fnet_w_in, fnet_w_out, conv_w_in, conv_w, conv_w_out):
    n_layers, n_ln, d = ln_g.shape
    w = {
        "wg": ffn_w_gate.astype(BF16),
        "wu": ffn_w_up.astype(BF16),
        "wd": ffn_w_down.astype(BF16),
        "ln_g": ln_g.reshape(n_layers, n_ln, 1, d),
        "ln_b": ln_b.reshape(n_layers, n_ln, 1, d),
        "fnet_in": fnet_w_in.astype(BF16),
        "fnet_out": fnet_w_out.astype(BF16),
        "conv_in": conv_w_in.astype(BF16),
        "conv_w": conv_w,
        "conv_out": conv_w_out.astype(BF16),
    }
    return (_trunk(x_prompt, w), _trunk(x_sample, w))
```

```python
import functools

import numpy as np
import jax
import jax.numpy as jnp
from jax import lax
from jax.experimental import pallas as pl
from jax.experimental.pallas import tpu as pltpu

D_MODEL = 1024
D_FF = 2816
DEPTH = 2
FNET_GROUPS = 8
FNET_GROUP_DIM = D_MODEL // FNET_GROUPS
ALPHA = float((2 * DEPTH) ** 0.25)
LN_EPS = 1e-5

F32 = jnp.float32
BF16 = jnp.bfloat16

V7X_VMEM_BYTES = 64 * 1024 * 1024
VMEM_LIMIT_BYTES = V7X_VMEM_BYTES - 8 * 1024 * 1024
SUBLANES = 8
BF16_ROWS = 16
LANES = 128

DFT_N2 = 128
FFN_TOKENS = SUBLANES * DFT_N2
FFN_SUB_TOKENS = 256
CONV_TOKENS = 1024
CONV_SUB_TOKENS = 256
FNET_TOKENS = 1024
FNET_SUB_TOKENS = 256


def _layer_norm(z, g, b):
    mu = jnp.mean(z, axis=-1, keepdims=True)
    zc = z - mu
    var = jnp.mean(zc * zc, axis=-1, keepdims=True)
    return zc * lax.rsqrt(var + LN_EPS) * g + b


def _dot(a, b):
    return jnp.dot(a, b, preferred_element_type=F32)


def _resident(arr, *lead):
    rest = arr.shape[len(lead):]
    index = tuple(lead) + (0,) * len(rest)
    return pl.BlockSpec((None,) * len(lead) + rest, lambda *_: index, pipeline_mode=pl.Buffered(1))


def _params(n_axes):
    return pltpu.CompilerParams(
        dimension_semantics=("arbitrary",) * n_axes,
        vmem_limit_bytes=VMEM_LIMIT_BYTES,
    )


def _ffn_kernel(x_ref, wg_ref, wu_ref, wd_ref, g_ref, b_ref, o_ref, *, swapped_out):
    n2, sub = DFT_N2, FFN_SUB_TOKENS
    span = sub // SUBLANES
    wg, wu, wd = wg_ref[...], wu_ref[...], wd_ref[...]
    g, b = g_ref[...], b_ref[...]
    for s in range(FFN_TOKENS // sub):
        if swapped_out:
            xg = jnp.stack([x_ref[r * n2 + s * span:r * n2 + (s + 1) * span, :]
                            for r in range(SUBLANES)], axis=0)
            x = jnp.swapaxes(xg, 0, 1).reshape(sub, xg.shape[-1])
        else:
            x = x_ref[s * sub:(s + 1) * sub, :]
        xb = x.astype(BF16)
        gate = _dot(xb, wg)
        up = _dot(xb, wu)
        h = (gate * jax.nn.sigmoid(gate) * up).astype(BF16)
        y = _layer_norm(ALPHA * x + 0.5 * _dot(h, wd), g, b)
        if swapped_out:
            o_ref[s * span:(s + 1) * span] = y.reshape(span, SUBLANES, y.shape[-1])
        else:
            o_ref[s * sub:(s + 1) * sub, :] = y


def _ffn_ln(x2d, w, layer, half, *, bsz, seq_len, swapped_out=False):
    d = D_MODEL
    n = bsz * seq_len
    tm = FFN_TOKENS
    n1 = seq_len // DFT_N2
    tiles_per_seq = seq_len // tm
    assert seq_len % tm == 0 and n1 == tiles_per_seq * SUBLANES
    nat = pl.BlockSpec((tm, d), lambda i: (i, 0))
    swp = pl.BlockSpec((None, DFT_N2, SUBLANES, d),
                       lambda i: (i // tiles_per_seq, 0, i % tiles_per_seq, 0))
    nat_shape = jax.ShapeDtypeStruct((n, d), F32)
    swp_shape = jax.ShapeDtypeStruct((bsz, DFT_N2, n1, d), F32)
    ln = 2 * half
    return pl.pallas_call(
        functools.partial(_ffn_kernel, swapped_out=swapped_out),
        grid=(n // tm,),
        in_specs=[nat, _resident(w["wg"], layer, half), _resident(w["wu"], layer, half),
                  _resident(w["wd"], layer, half), _resident(w["ln_g"], layer, ln),
                  _resident(w["ln_b"], layer, ln)],
        out_specs=swp if swapped_out else nat,
        out_shape=swp_shape if swapped_out else nat_shape,
        compiler_params=_params(1),
        name="ffn_ln",
    )(x2d, w["wg"], w["wu"], w["wd"], w["ln_g"], w["ln_b"])


def _conv_kernel(x_ref, xp_ref, xn_ref, win_ref, cw_ref, wout_ref, g_ref, b_ref, o_ref,
                 *, tiles_per_seq):
    tm, d = x_ref.shape
    sub = CONV_SUB_TOKENS
    n_sub = tm // sub
    i = pl.program_id(0)
    first = (i % tiles_per_seq) == 0
    last = (i % tiles_per_seq) == tiles_per_seq - 1
    xp = jnp.where(first, 0.0, xp_ref[...])
    xn = jnp.where(last, 0.0, xn_ref[...])
    w_b, w_c, w_h = win_ref[:, :d], win_ref[:, d:2 * d], win_ref[:, 2 * d:]
    wout, cw, g, b = wout_ref[...], cw_ref[...], g_ref[...], b_ref[...]
    rows = sub + 2 * SUBLANES
    vs = []
    for s in range(n_sub):
        parts = [x_ref[s * sub:(s + 1) * sub, :]]
        if s == 0:
            parts.insert(0, xp)
        if s == n_sub - 1:
            parts.append(xn)
        xc = jnp.concatenate(parts, axis=0).astype(BF16)
        vs.append(_dot(xc, w_c) * _dot(xc, w_h))
    for s in range(n_sub):
        lo, hi = s * sub, (s + 1) * sub
        x = x_ref[lo:hi, :]
        pieces = [vs[s]]
        if s > 0:
            pieces.insert(0, vs[s - 1][-SUBLANES:])
        if s < n_sub - 1:
            pieces.append(vs[s + 1][:SUBLANES])
        v = jnp.concatenate(pieces, axis=0)
        gate_b = _dot(x.astype(BF16), w_b)
        v_prev = pltpu.roll(v, 1, axis=0)[SUBLANES:SUBLANES + sub]
        v_next = pltpu.roll(v, rows - 1, axis=0)[SUBLANES:SUBLANES + sub]
        v_mid = v[SUBLANES:SUBLANES + sub]
        conv = v_prev * cw[0:1] + v_mid * cw[1:2] + v_next * cw[2:3]
        y = _dot((gate_b * conv).astype(BF16), wout)
        o_ref[lo:hi, :] = _layer_norm(ALPHA * x + y, g, b)


def _conv_ln(x2d, seq_len, w, layer, j):
    n, d = x2d.shape
    tm = CONV_TOKENS
    assert seq_len % tm == 0 and tm % CONV_SUB_TOKENS == 0
    halo_per_tile = tm // SUBLANES
    n_halo_blocks = n // SUBLANES
    row = pl.BlockSpec((tm, d), lambda i: (i, 0))
    prev = pl.BlockSpec((SUBLANES, d), lambda i: (jnp.maximum(i * halo_per_tile - 1, 0), 0))
    nxt = pl.BlockSpec((SUBLANES, d),
                       lambda i: (jnp.minimum((i + 1) * halo_per_tile, n_halo_blocks - 1), 0))
    return pl.pallas_call(
        functools.partial(_conv_kernel, tiles_per_seq=seq_len // tm),
        grid=(n // tm,),
        in_specs=[row, prev, nxt, _resident(w["conv_in"], j), _resident(w["conv_w"], j),
                  _resident(w["conv_out"], j), _resident(w["ln_g"], layer, 1),
                  _resident(w["ln_b"], layer, 1)],
        out_specs=row,
        out_shape=jax.ShapeDtypeStruct((n, d), F32),
        compiler_params=_params(1),
        name="conv_ln",
    )(x2d, x2d, x2d, w["conv_in"], w["conv_w"], w["conv_out"], w["ln_g"], w["ln_b"])


def _dft_cos_sin(rows, cols, n):
    ang = 2.0 * np.pi * ((rows[:, None] * cols[None, :]) % n) / n
    return np.cos(ang), np.sin(ang)


def _fnet_constants(seq_len):
    n2 = DFT_N2
    n1 = seq_len // n2
    m = n2 // n1
    gd = np.arange(FNET_GROUP_DIM)
    cg, sg = _dft_cos_sin(gd, gd, FNET_GROUP_DIM)
    gcs = np.concatenate([cg, sg], axis=1) / np.sqrt(FNET_GROUP_DIM)
    i1 = np.arange(n1)
    c1, s1 = _dft_cos_sin(i1, i1, n1)
    m1 = np.block([[c1, -s1], [s1, c1]]) / np.sqrt(n1)
    k2 = (m * np.arange(n1)[None, :] + np.arange(m)[:, None]).reshape(-1)
    c2, s2 = _dft_cos_sin(k2, np.arange(n2), n2)
    m2 = np.concatenate([c2, -s2], axis=1) / np.sqrt(n2)
    ang = 2.0 * np.pi * ((np.arange(n2)[:, None] * i1[None, :]) % seq_len) / seq_len
    twc = jnp.broadcast_to(jnp.asarray(np.cos(ang), F32)[:, :, None], (n2, n1, LANES))
    tws = jnp.broadcast_to(jnp.asarray(np.sin(ang), F32)[:, :, None], (n2, n1, LANES))
    return (jnp.asarray(gcs, F32).astype(BF16), jnp.asarray(m1, F32).astype(BF16),
            jnp.asarray(m2, F32).astype(BF16), twc, tws)


def _fnet_stage1_kernel(x_ref, win_ref, gcs_ref, m1_ref, twc_ref, tws_ref, tr_ref, ti_ref):
    cols, n1, d = x_ref.shape
    gd = FNET_GROUP_DIM
    sub_cols = FNET_SUB_TOKENS // n1
    win, gcs, m1 = win_ref[...], gcs_ref[...], m1_ref[...]
    res, ims = [], []
    for c0 in range(0, cols, sub_cols):
        xs = x_ref[c0:c0 + sub_cols].reshape(sub_cols * n1, d).astype(BF16)
        u = _dot(xs, win).astype(BF16)
        pq = [_dot(u[:, g * gd:(g + 1) * gd], gcs) for g in range(FNET_GROUPS)]
        p = jnp.concatenate([t[:, :gd] for t in pq], axis=1).astype(BF16)
        q = jnp.concatenate([t[:, gd:] for t in pq], axis=1).astype(BF16)
        for j in range(sub_cols):
            rhs = jnp.concatenate([p[j * n1:(j + 1) * n1], q[j * n1:(j + 1) * n1]], axis=0)
            t = _dot(m1, rhs)
            t_re, t_im = t[:n1], t[n1:]
            c = jnp.concatenate([twc_ref[c0 + j]] * (d // LANES), axis=1)
            s = jnp.concatenate([tws_ref[c0 + j]] * (d // LANES), axis=1)
            res.append((t_re * c - t_im * s).astype(BF16))
            ims.append((t_re * s + t_im * c).astype(BF16))
            if len(res) == BF16_ROWS:
                lo = c0 + j + 1 - BF16_ROWS
                tr_ref[:, lo:lo + BF16_ROWS, :] = jnp.swapaxes(jnp.stack(res, axis=0), 0, 1)
                ti_ref[:, lo:lo + BF16_ROWS, :] = jnp.swapaxes(jnp.stack(ims, axis=0), 0, 1)
                res, ims = [], []


def _fnet_stage2_kernel(tr_ref, ti_ref, x_ref, m2_ref, wout_ref, g_ref, b_ref, o_ref):
    m, cols, n1, d = x_ref.shape
    m2, wout, g, b = m2_ref[...], wout_ref[...], g_ref[...], b_ref[...]
    n2 = m * n1
    per = FNET_SUB_TOKENS // n2
    ys = [[None] * cols for _ in range(m)]
    for k0 in range(0, cols, per):
        fs = [_dot(m2, jnp.concatenate([tr_ref[k], ti_ref[k]], axis=0)) for k in range(k0, k0 + per)]
        mix = _dot(jnp.concatenate(fs, axis=0).astype(BF16), wout)
        for kk in range(per):
            for h in range(m):
                z = ALPHA * x_ref[h, k0 + kk] + mix[kk * n2 + h * n1:kk * n2 + (h + 1) * n1]
                ys[h][k0 + kk] = _layer_norm(z, g, b)
    for h in range(m):
        o_ref[:, h, :, :] = jnp.swapaxes(jnp.stack(ys[h], axis=0), 0, 1)


def _fnet_ln(xs4, w, layer, j):
    bsz, n2, n1, d = xs4.shape
    assert n2 == DFT_N2 and n2 % n1 == 0 and d == D_MODEL
    m = n2 // n1
    gcs, m1, m2, twc, tws = _fnet_constants(n1 * n2)

    ca = max(FNET_TOKENS // n1, BF16_ROWS)
    assert n2 % ca == 0 and ca % BF16_ROWS == 0 and FNET_SUB_TOKENS % n1 == 0
    x_blk = pl.BlockSpec((None, ca, n1, d), lambda bi, c: (bi, c, 0, 0))
    tw_blk = pl.BlockSpec((ca, n1, LANES), lambda bi, c: (c, 0, 0))
    t_out = pl.BlockSpec((None, n1, ca, d), lambda bi, c: (bi, 0, c, 0))
    t_shape = jax.ShapeDtypeStruct((bsz, n1, n2, d), BF16)
    t_re, t_im = pl.pallas_call(
        _fnet_stage1_kernel,
        grid=(bsz, n2 // ca),
        in_specs=[x_blk, _resident(w["fnet_in"], j), _resident(gcs), _resident(m1), tw_blk, tw_blk],
        out_specs=[t_out, t_out],
        out_shape=[t_shape, t_shape],
        compiler_params=_params(2),
        name="fnet_stage1",
    )(xs4, w["fnet_in"], gcs, m1, twc, tws)

    cb = FNET_TOKENS // n2
    assert n1 % cb == 0 and cb % SUBLANES == 0 and FNET_SUB_TOKENS % n2 == 0 and cb % (FNET_SUB_TOKENS // n2) == 0
    xs5 = xs4.reshape(bsz, m, n1, n1, d)
    t_in = pl.BlockSpec((None, cb, n2, d), lambda bi, k: (bi, k, 0, 0))
    r_blk = pl.BlockSpec((None, m, cb, n1, d), lambda bi, k: (bi, 0, k, 0, 0))
    o_blk = pl.BlockSpec((None, n1, m, cb, d), lambda bi, k: (bi, 0, 0, k, 0))
    out = pl.pallas_call(
        _fnet_stage2_kernel,
        grid=(bsz, n1 // cb),
        in_specs=[t_in, t_in, r_blk, _resident(m2), _resident(w["fnet_out"], j),
                  _resident(w["ln_g"], layer, 1), _resident(w["ln_b"], layer, 1)],
        out_specs=o_blk,
        out_shape=jax.ShapeDtypeStruct((bsz, n1, m, n1, d), F32),
        compiler_params=_params(2),
        name="fnet_stage2",
    )(t_re, t_im, xs5, m2, w["fnet_out"], w["ln_g"], w["ln_b"])
    return out.reshape(bsz, n1 * n2, d)


def _trunk(x, w):
    bsz, s, d = x.shape
    assert DEPTH == 2 and d == D_MODEL
    ffn = functools.partial(_ffn_ln, bsz=bsz, seq_len=s)
    x = x.reshape(bsz * s, d)
    xs = ffn(x, w, 0, 0, swapped_out=True)
    x = _fnet_ln(xs, w, 0, 0).reshape(bsz * s, d)
    x = ffn(x, w, 0, 1)
    x = ffn(x, w, 1, 0)
    x = _conv_ln(x, s, w, 1, 0)
    x = ffn(x, w, 1, 1)
    return x.reshape(bsz, s, d)


def kernel(x_prompt, x_sample, ffn_w_gate, ffn_w_up, ffn_w_down, ln_g, ln_b,
           fnet_w_in, fnet_w_out, conv_w_in, conv_w, conv_w_out):
    n_layers, n_ln, d = ln_g.shape
    w = {
        "wg": ffn_w_gate.astype(BF16),
        "wu": ffn_w_up.astype(BF16),
        "wd": ffn_w_down.astype(BF16),
        "ln_g": ln_g.reshape(n_layers, n_ln, 1, d),
        "ln_b": ln_b.reshape(n_layers, n_ln, 1, d),
        "fnet_in": fnet_w_in.astype(BF16),
        "fnet_out": fnet_w_out.astype(BF16),
        "conv_in": conv_w_in.astype(BF16),
        "conv_w": conv_w,
        "conv_out": conv_w_out.astype(BF16),
    }
    return (_trunk(x_prompt, w), _trunk(x_sample, w))
```

```python
import functools

import numpy as np
import jax
import jax.numpy as jnp
from jax import lax
from jax.experimental import pallas as pl
from jax.experimental.pallas import tpu as pltpu

D_MODEL = 1024
D_FF = 2816
DEPTH = 2
FNET_GROUPS = 8
FNET_GROUP_DIM = D_MODEL // FNET_GROUPS
ALPHA = float((2 * DEPTH) ** 0.25)
LN_EPS = 1e-5

F32 = jnp.float32
BF16 = jnp.bfloat16

V7X_VMEM_BYTES = 64 * 1024 * 1024
VMEM_LIMIT_BYTES = V7X_VMEM_BYTES - 8 * 1024 * 1024
SUBLANES = 8
BF16_ROWS = 16
LANES = 128

DFT_N2 = 128
FFN_TOKENS = SUBLANES * DFT_N2
FFN_SUB_TOKENS = 256
CONV_TOKENS = 2048
CONV_SUB_TOKENS = 256
FNET_TOKENS = 1024
FNET_SUB_TOKENS = 256


def _layer_norm(z, g, b):
    mu = jnp.mean(z, axis=-1, keepdims=True)
    zc = z - mu
    var = jnp.mean(zc * zc, axis=-1, keepdims=True)
    return zc * lax.rsqrt(var + LN_EPS) * g + b


def _dot(a, b):
    return jnp.dot(a, b, preferred_element_type=F32)


def _resident(arr, *lead):
    rest = arr.shape[len(lead):]
    index = tuple(lead) + (0,) * len(rest)
    return pl.BlockSpec((None,) * len(lead) + rest, lambda *_: index, pipeline_mode=pl.Buffered(1))


def _params(n_axes):
    return pltpu.CompilerParams(
        dimension_semantics=("arbitrary",) * n_axes,
        vmem_limit_bytes=VMEM_LIMIT_BYTES,
    )


def _ffn_kernel(x_ref, wg_ref, wu_ref, wd_ref, g_ref, b_ref, o_ref, *, swapped_out):
    n2, sub = DFT_N2, FFN_SUB_TOKENS
    span = sub // SUBLANES
    wg, wu, wd = wg_ref[...], wu_ref[...], wd_ref[...]
    g, b = g_ref[...], b_ref[...]
    for s in range(FFN_TOKENS // sub):
        if swapped_out:
            xg = jnp.stack([x_ref[r * n2 + s * span:r * n2 + (s + 1) * span, :]
                            for r in range(SUBLANES)], axis=0)
            x = jnp.swapaxes(xg, 0, 1).reshape(sub, xg.shape[-1])
        else:
            x = x_ref[s * sub:(s + 1) * sub, :]
        xb = x.astype(BF16)
        gate = _dot(xb, wg)
        up = _dot(xb, wu)
        h = (gate * jax.nn.sigmoid(gate) * up).astype(BF16)
        y = _layer_norm(ALPHA * x + 0.5 * _dot(h, wd), g, b)
        if swapped_out:
            o_ref[s * span:(s + 1) * span] = y.reshape(span, SUBLANES, y.shape[-1])
        else:
            o_ref[s * sub:(s + 1) * sub, :] = y


def _ffn_ln(x2d, w, layer, half, *, bsz, seq_len, swapped_out=False):
    d = D_MODEL
    n = bsz * seq_len
    tm = FFN_TOKENS
    n1 = seq_len // DFT_N2
    tiles_per_seq = seq_len // tm
    assert seq_len % tm == 0 and n1 == tiles_per_seq * SUBLANES
    nat = pl.BlockSpec((tm, d), lambda i: (i, 0))
    swp = pl.BlockSpec((None, DFT_N2, SUBLANES, d),
                       lambda i: (i // tiles_per_seq, 0, i % tiles_per_seq, 0))
    nat_shape = jax.ShapeDtypeStruct((n, d), F32)
    swp_shape = jax.ShapeDtypeStruct((bsz, DFT_N2, n1, d), F32)
    ln = 2 * half
    return pl.pallas_call(
        functools.partial(_ffn_kernel, swapped_out=swapped_out),
        grid=(n // tm,),
        in_specs=[nat, _resident(w["wg"], layer, half), _resident(w["wu"], layer, half),
                  _resident(w["wd"], layer, half), _resident(w["ln_g"], layer, ln),
                  _resident(w["ln_b"], layer, ln)],
        out_specs=swp if swapped_out else nat,
        out_shape=swp_shape if swapped_out else nat_shape,
        compiler_params=_params(1),
        name="ffn_ln",
    )(x2d, w["wg"], w["wu"], w["wd"], w["ln_g"], w["ln_b"])


def _conv_kernel(x_ref, xp_ref, xn_ref, win_ref, cw_ref, wout_ref, g_ref, b_ref, o_ref,
                 *, tiles_per_seq):
    tm, d = x_ref.shape
    sub = CONV_SUB_TOKENS
    n_sub = tm // sub
    i = pl.program_id(0)
    first = (i % tiles_per_seq) == 0
    last = (i % tiles_per_seq) == tiles_per_seq - 1
    xp = jnp.where(first, 0.0, xp_ref[...])
    xn = jnp.where(last, 0.0, xn_ref[...])
    w_b, w_c, w_h = win_ref[:, :d], win_ref[:, d:2 * d], win_ref[:, 2 * d:]
    wout, cw, g, b = wout_ref[...], cw_ref[...], g_ref[...], b_ref[...]
    rows = sub + 2 * SUBLANES
    vs = []
    for s in range(n_sub):
        parts = [x_ref[s * sub:(s + 1) * sub, :]]
        if s == 0:
            parts.insert(0, xp)
        if s == n_sub - 1:
            parts.append(xn)
        xc = jnp.concatenate(parts, axis=0).astype(BF16)
        vs.append(_dot(xc, w_c) * _dot(xc, w_h))
    for s in range(n_sub):
        lo, hi = s * sub, (s + 1) * sub
        x = x_ref[lo:hi, :]
        pieces = [vs[s]]
        if s > 0:
            pieces.insert(0, vs[s - 1][-SUBLANES:])
        if s < n_sub - 1:
            pieces.append(vs[s + 1][:SUBLANES])
        v = jnp.concatenate(pieces, axis=0)
        gate_b = _dot(x.astype(BF16), w_b)
        v_prev = pltpu.roll(v, 1, axis=0)[SUBLANES:SUBLANES + sub]
        v_next = pltpu.roll(v, rows - 1, axis=0)[SUBLANES:SUBLANES + sub]
        v_mid = v[SUBLANES:SUBLANES + sub]
        conv = v_prev * cw[0:1] + v_mid * cw[1:2] + v_next * cw[2:3]
        y = _dot((gate_b * conv).astype(BF16), wout)
        o_ref[lo:hi, :] = _layer_norm(ALPHA * x + y, g, b)


def _conv_ln(x2d, seq_len, w, layer, j):
    n, d = x2d.shape
    tm = CONV_TOKENS
    assert seq_len % tm == 0 and tm % CONV_SUB_TOKENS == 0
    halo_per_tile = tm // SUBLANES
    n_halo_blocks = n // SUBLANES
    row = pl.BlockSpec((tm, d), lambda i: (i, 0))
    prev = pl.BlockSpec((SUBLANES, d), lambda i: (jnp.maximum(i * halo_per_tile - 1, 0), 0))
    nxt = pl.BlockSpec((SUBLANES, d),
                       lambda i: (jnp.minimum((i + 1) * halo_per_tile, n_halo_blocks - 1), 0))
    return pl.pallas_call(
        functools.partial(_conv_kernel, tiles_per_seq=seq_len // tm),
        grid=(n // tm,),
        in_specs=[row, prev, nxt, _resident(w["conv_in"], j), _resident(w["conv_w"], j),
                  _resident(w["conv_out"], j), _resident(w["ln_g"], layer, 1),
                  _resident(w["ln_b"], layer, 1)],
        out_specs=row,
        out_shape=jax.ShapeDtypeStruct((n, d), F32),
        compiler_params=_params(1),
        name="conv_ln",
    )(x2d, x2d, x2d, w["conv_in"], w["conv_w"], w["conv_out"], w["ln_g"], w["ln_b"])


def _dft_cos_sin(rows, cols, n):
    ang = 2.0 * np.pi * ((rows[:, None] * cols[None, :]) % n) / n
    return np.cos(ang), np.sin(ang)


def _fnet_constants(seq_len):
    n2 = DFT_N2
    n1 = seq_len // n2
    m = n2 // n1
    gd = np.arange(FNET_GROUP_DIM)
    cg, sg = _dft_cos_sin(gd, gd, FNET_GROUP_DIM)
    gcs = np.concatenate([cg, sg], axis=1) / np.sqrt(FNET_GROUP_DIM)
    i1 = np.arange(n1)
    c1, s1 = _dft_cos_sin(i1, i1, n1)
    m1 = np.block([[c1, -s1], [s1, c1]]) / np.sqrt(n1)
    k2 = (m * np.arange(n1)[None, :] + np.arange(m)[:, None]).reshape(-1)
    c2, s2 = _dft_cos_sin(k2, np.arange(n2), n2)
    m2 = np.concatenate([c2, -s2], axis=1) / np.sqrt(n2)
    ang = 2.0 * np.pi * ((np.arange(n2)[:, None] * i1[None, :]) % seq_len) / seq_len
    twc = jnp.broadcast_to(jnp.asarray(np.cos(ang), F32)[:, :, None], (n2, n1, LANES))
    tws = jnp.broadcast_to(jnp.asarray(np.sin(ang), F32)[:, :, None], (n2, n1, LANES))
    return (jnp.asarray(gcs, F32).astype(BF16), jnp.asarray(m1, F32).astype(BF16),
            jnp.asarray(m2, F32).astype(BF16), twc, tws)


def _fnet_stage1_kernel(x_ref, win_ref, gcs_ref, m1_ref, twc_ref, tws_ref, tr_ref, ti_ref):
    cols, n1, d = x_ref.shape
    gd = FNET_GROUP_DIM
    sub_cols = FNET_SUB_TOKENS // n1
    win, gcs, m1 = win_ref[...], gcs_ref[...], m1_ref[...]
    res, ims = [], []
    for c0 in range(0, cols, sub_cols):
        xs = x_ref[c0:c0 + sub_cols].reshape(sub_cols * n1, d).astype(BF16)
        u = _dot(xs, win).astype(BF16)
        pq = [_dot(u[:, g * gd:(g + 1) * gd], gcs) for g in range(FNET_GROUPS)]
        p = jnp.concatenate([t[:, :gd] for t in pq], axis=1).astype(BF16)
        q = jnp.concatenate([t[:, gd:] for t in pq], axis=1).astype(BF16)
        for j in range(sub_cols):
            rhs = jnp.concatenate([p[j * n1:(j + 1) * n1], q[j * n1:(j + 1) * n1]], axis=0)
            t = _dot(m1, rhs)
            t_re, t_im = t[:n1], t[n1:]
            c = jnp.concatenate([twc_ref[c0 + j]] * (d // LANES), axis=1)
            s = jnp.concatenate([tws_ref[c0 + j]] * (d // LANES), axis=1)
            res.append((t_re * c - t_im * s).astype(BF16))
            ims.append((t_re * s + t_im * c).astype(BF16))
            if len(res) == BF16_ROWS:
                lo = c0 + j + 1 - BF16_ROWS
                tr_ref[:, lo:lo + BF16_ROWS, :] = jnp.swapaxes(jnp.stack(res, axis=0), 0, 1)
                ti_ref[:, lo:lo + BF16_ROWS, :] = jnp.swapaxes(jnp.stack(ims, axis=0), 0, 1)
                res, ims = [], []


def _fnet_stage2_kernel(tr_ref, ti_ref, x_ref, m2_ref, wout_ref, g_ref, b_ref, o_ref):
    m, cols, n1, d = x_ref.shape
    m2, wout, g, b = m2_ref[...], wout_ref[...], g_ref[...], b_ref[...]
    n2 = m * n1
    per = FNET_SUB_TOKENS // n2
    ys = [[None] * cols for _ in range(m)]
    for k0 in range(0, cols, per):
        fs = [_dot(m2, jnp.concatenate([tr_ref[k], ti_ref[k]], axis=0)) for k in range(k0, k0 + per)]
        mix = _dot(jnp.concatenate(fs, axis=0).astype(BF16), wout)
        for kk in range(per):
            for h in range(m):
                z = ALPHA * x_ref[h, k0 + kk] + mix[kk * n2 + h * n1:kk * n2 + (h + 1) * n1]
                ys[h][k0 + kk] = _layer_norm(z, g, b)
    for h in range(m):
        o_ref[:, h, :, :] = jnp.swapaxes(jnp.stack(ys[h], axis=0), 0, 1)


def _fnet_ln(xs4, w, layer, j):
    bsz, n2, n1, d = xs4.shape
    assert n2 == DFT_N2 and n2 % n1 == 0 and d == D_MODEL
    m = n2 // n1
    gcs, m1, m2, twc, tws = _fnet_constants(n1 * n2)

    ca = max(FNET_TOKENS // n1, BF16_ROWS)
    assert n2 % ca == 0 and ca % BF16_ROWS == 0 and FNET_SUB_TOKENS % n1 == 0
    x_blk = pl.BlockSpec((None, ca, n1, d), lambda bi, c: (bi, c, 0, 0))
    tw_blk = pl.BlockSpec((ca, n1, LANES), lambda bi, c: (c, 0, 0))
    t_out = pl.BlockSpec((None, n1, ca, d), lambda bi, c: (bi, 0, c, 0))
    t_shape = jax.ShapeDtypeStruct((bsz, n1, n2, d), BF16)
    t_re, t_im = pl.pallas_call(
        _fnet_stage1_kernel,
        grid=(bsz, n2 // ca),
        in_specs=[x_blk, _resident(w["fnet_in"], j), _resident(gcs), _resident(m1), tw_blk, tw_blk],
        out_specs=[t_out, t_out],
        out_shape=[t_shape, t_shape],
        compiler_params=_params(2),
        name="fnet_stage1",
    )(xs4, w["fnet_in"], gcs, m1, twc, tws)

    cb = FNET_TOKENS // n2
    assert n1 % cb == 0 and cb % SUBLANES == 0 and FNET_SUB_TOKENS % n2 == 0 and cb % (FNET_SUB_TOKENS // n2) == 0
    xs5 = xs4.reshape(bsz, m, n1, n1, d)
    t_in = pl.BlockSpec((None, cb, n2, d), lambda bi, k: (bi, k, 0, 0))
    r_blk = pl.BlockSpec((None, m, cb, n1, d), lambda bi, k: (bi, 0, k, 0, 0))
    o_blk = pl.BlockSpec((None, n1, m, cb, d), lambda bi, k: (bi, 0, 0, k, 0))
    out = pl.pallas_call(
        _fnet_stage2_kernel,
        grid=(bsz, n1 // cb),
        in_specs=[t_in, t_in, r_blk, _resident(m2), _resident(w["fnet_out"], j),
                  _resident(w["ln_g"], layer, 1), _resident(w["ln_b"], layer, 1)],
        out_specs=o_blk,
        out_shape=jax.ShapeDtypeStruct((bsz, n1, m, n1, d), F32),
        compiler_params=_params(2),
        name="fnet_stage2",
    )(t_re, t_im, xs5, m2, w["fnet_out"], w["ln_g"], w["ln_b"])
    return out.reshape(bsz, n1 * n2, d)


def _trunk(x, w):
    bsz, s, d = x.shape
    assert DEPTH == 2 and d == D_MODEL
    ffn = functools.partial(_ffn_ln, bsz=bsz, seq_len=s)
    x = x.reshape(bsz * s, d)
    xs = ffn(x, w, 0, 0, swapped_out=True)
    x = _fnet_ln(xs, w, 0, 0).reshape(bsz * s, d)
    x = ffn(x, w, 0, 1)
    x = ffn(x, w, 1, 0)
    x = _conv_ln(x, s, w, 1, 0)
    x = ffn(x, w, 1, 1)
    return x.reshape(bsz, s, d)


def kernel(x_prompt, x_sample, ffn_w_gate, ffn_w_up, ffn_w_down, ln_g, ln_b,
           fnet_w_in, fnet_w_out, conv_w_in, conv_w, conv_w_out):
    n_layers, n_ln, d = ln_g.shape
    w = {
        "wg": ffn_w_gate.astype(BF16),
        "wu": ffn_w_up.astype(BF16),
        "wd": ffn_w_down.astype(BF16),
        "ln_g": ln_g.reshape(n_layers, n_ln, 1, d),
        "ln_b": ln_b.reshape(n_layers, n_ln, 1, d),
        "fnet_in": fnet_w_in.astype(BF16),
        "fnet_out": fnet_w_out.astype(BF16),
        "conv_in": conv_w_in.astype(BF16),
        "conv_w": conv_w,
        "conv_out": conv_w_out.astype(BF16),
    }
    return (_trunk(x_prompt, w), _trunk(x_sample, w))
```
